```python
import math
import jax, jax.numpy as jnp
from jax import lax
import numpy as np

D_MODEL = 2048
BATCH = 1
SEQ = 16384
DEPTH = 1

HEAD_DIM = 128
GRID_W = 64
Q_BLOCK = 128
A_Q_HEADS = 8
A_KV_HEADS = 2
A_GROUP = A_Q_HEADS // A_KV_HEADS
B_HEADS = 8
WIN_ROWS_MAX = 8
WIN_COLS = 16
A_WIDTH = A_Q_HEADS * HEAD_DIM
B_WIDTH = B_HEADS * HEAD_DIM
MIX_WIDTH = A_WIDTH + B_WIDTH
A_KV_WIDTH = A_KV_HEADS * HEAD_DIM
IN_WIDTH = A_WIDTH + 2 * A_KV_WIDTH + 3 * B_WIDTH
D_FF = 4 * D_MODEL
PLE_DIM = 256
ROPE_THETA = 10000.0
ROPE_AXIS_DIM = HEAD_DIM // 2
NORM_EPS = 1e-6

kernel_name = "hybrid_gqa_axialrope_neighbourhood_attn_layer"


def rms_norm(x, g):
    xf = x.astype(jnp.float32)
    y = xf * lax.rsqrt(jnp.mean(xf * xf, axis=-1, keepdims=True) + NORM_EPS)
    return (y * g.astype(jnp.float32)).astype(x.dtype)


def rope_half(x, ang):
    n = ang.shape[-1]
    shape = (1, ang.shape[0]) + (1,) * (x.ndim - 3) + (n,)
    cos = jnp.cos(ang).reshape(shape)
    sin = jnp.sin(ang).reshape(shape)
    xf = x.astype(jnp.float32)
    x1, x2 = xf[..., :n], xf[..., n:]
    return jnp.concatenate([x1 * cos - x2 * sin, x2 * cos + x1 * sin], axis=-1).astype(x.dtype)


def axial_rope(x, ang_row, ang_col):
    return jnp.concatenate([rope_half(x[..., :ROPE_AXIS_DIM], ang_row),
                            rope_half(x[..., ROPE_AXIS_DIM:], ang_col)], axis=-1)


def to_blocks(a):
    b, s = a.shape[0], a.shape[1]
    a = a.reshape((b, s // Q_BLOCK, Q_BLOCK) + a.shape[2:])
    return jnp.moveaxis(a, 1, 0)


def from_blocks(a):
    a = jnp.moveaxis(a, 0, 1)
    return a.reshape((a.shape[0], a.shape[1] * a.shape[2]) + a.shape[3:])


def global_gqa(q, k, v):
    scale = 1.0 / math.sqrt(HEAD_DIM)

    def block(qb):
        s = jnp.einsum('bqkgd,bskd->bkgqs', qb, k).astype(jnp.float32) * scale
        w = jax.nn.softmax(s, axis=-1).astype(v.dtype)
        return jnp.einsum('bkgqs,bskd->bqkgd', w, v)

    o = lax.map(block, to_blocks(q))
    return from_blocks(o)


def neighbourhood_attn(q, k, v, rpb, rows):
    s_len = q.shape[1]
    win_r = min(WIN_ROWS_MAX, rows)
    scale = 1.0 / math.sqrt(HEAD_DIM)
    t_blocks = jnp.arange(s_len, dtype=jnp.int32).reshape(s_len // Q_BLOCK, Q_BLOCK)
    ar_r = jnp.arange(win_r, dtype=jnp.int32)
    ar_c = jnp.arange(WIN_COLS, dtype=jnp.int32)

    def block(args):
        qb, t = args
        r = t // GRID_W
        c = t % GRID_W
        r0 = jnp.clip(r - win_r // 2, 0, rows - win_r)
        c0 = jnp.clip(c - WIN_COLS // 2, 0, GRID_W - WIN_COLS)
        kr = r0[:, None] + ar_r[None, :]
        kc = c0[:, None] + ar_c[None, :]
        idx = (kr[:, :, None] * GRID_W + kc[:, None, :]).reshape(Q_BLOCK, win_r * WIN_COLS)
        dr = kr - r[:, None] + (WIN_ROWS_MAX - 1)
        dc = kc - c[:, None] + (WIN_COLS - 1)
        bias = rpb[:, dr[:, :, None], dc[:, None, :]].reshape(B_HEADS, Q_BLOCK, win_r * WIN_COLS)
        kg = k[:, idx]
        vg = v[:, idx]
        s = jnp.einsum('bqhd,bqnhd->bhqn', qb, kg).astype(jnp.float32) * scale
        s = s + bias.astype(jnp.float32)[None]
        w = jax.nn.softmax(s, axis=-1).astype(vg.dtype)
        return jnp.einsum('bhqn,bqnhd->bqhd', w, vg)

    o = lax.map(block, (to_blocks(q), t_blocks))
    return from_blocks(o)


def setup_inputs(seed: int = 0) -> dict:
    key = jax.random.key(seed)
    ks = jax.random.split(key, 20)
    f32 = jnp.float32

    def w(k, shape, fan_in):
        return jax.random.normal(k, shape, f32) * (fan_in ** -0.5)

    def gain(k, shape):
        return 1.0 + 0.05 * jax.random.normal(k, shape, f32)

    return {
        "x": jax.random.normal(ks[0], (BATCH, SEQ, D_MODEL), f32),
        "p": jax.random.normal(ks[1], (DEPTH, BATCH, SEQ, PLE_DIM), f32),
        "pre_mix_norm": gain(ks[2], (DEPTH, D_MODEL)),
        "w_in": w(ks[3], (DEPTH, D_MODEL, IN_WIDTH), D_MODEL),
        "q_norm": gain(ks[4], (DEPTH, HEAD_DIM)),
        "k_norm": gain(ks[5], (DEPTH, HEAD_DIM)),
        "rel_pos_bias": 0.1 * jax.random.normal(ks[6], (DEPTH, B_HEADS, 2 * WIN_ROWS_MAX - 1, 2 * WIN_COLS - 1), f32),
        "w_o": w(ks[7], (DEPTH, MIX_WIDTH, D_MODEL), MIX_WIDTH),
        "post_mix_norm": gain(ks[8], (DEPTH, D_MODEL)),
        "pre_mlp_norm": gain(ks[9], (DEPTH, D_MODEL)),
        "w_up": w(ks[10], (DEPTH, D_MODEL, D_FF), D_MODEL),
        "w_down": w(ks[11], (DEPTH, D_FF, D_MODEL), D_FF),
        "post_mlp_norm": gain(ks[12], (DEPTH, D_MODEL)),
        "pre_ple_norm": gain(ks[13], (DEPTH, D_MODEL)),
        "w_ple_gate": w(ks[14], (DEPTH, D_MODEL, D_MODEL), D_MODEL),
        "w_ple_proj": w(ks[15], (DEPTH, PLE_DIM, D_MODEL), PLE_DIM),
        "post_ple_norm": gain(ks[16], (DEPTH, D_MODEL)),
    }


def reference(x, p, pre_mix_norm, w_in, q_norm, k_norm, rel_pos_bias, w_o, post_mix_norm,
              pre_mlp_norm, w_up, w_down, post_mlp_norm, pre_ple_norm, w_ple_gate, w_ple_proj,
              post_ple_norm):
    b, s_len, _ = x.shape
    rows = s_len // GRID_W
    t = jnp.arange(s_len, dtype=jnp.int32)
    row = (t // GRID_W).astype(jnp.float32)
    col = (t % GRID_W).astype(jnp.float32)
    n_freq = ROPE_AXIS_DIM // 2
    freqs = ROPE_THETA ** (-jnp.arange(n_freq, dtype=jnp.float32) / n_freq)
    ang_row = row[:, None] * freqs[None, :]
    ang_col = col[:, None] * freqs[None, :]

    h = x
    for i in range(DEPTH):
        xn = rms_norm(h, pre_mix_norm[i])
        proj = xn @ w_in[i]
        o0 = A_WIDTH
        o1 = o0 + A_KV_WIDTH
        o2 = o1 + A_KV_WIDTH
        o3 = o2 + B_WIDTH
        o4 = o3 + B_WIDTH
        qa = proj[..., :o0].reshape(b, s_len, A_KV_HEADS, A_GROUP, HEAD_DIM)
        ka = proj[..., o0:o1].reshape(b, s_len, A_KV_HEADS, HEAD_DIM)
        va = proj[..., o1:o2].reshape(b, s_len, A_KV_HEADS, HEAD_DIM)
        qb = proj[..., o2:o3].reshape(b, s_len, B_HEADS, HEAD_DIM)
        kb = proj[..., o3:o4].reshape(b, s_len, B_HEADS, HEAD_DIM)
        vb = proj[..., o4:].reshape(b, s_len, B_HEADS, HEAD_DIM)

        qa = axial_rope(rms_norm(qa, q_norm[i]), ang_row, ang_col)
        ka = axial_rope(rms_norm(ka, k_norm[i]), ang_row, ang_col)
        out_a = global_gqa(qa, ka, va).reshape(b, s_len, A_WIDTH)
        out_b = neighbourhood_attn(qb, kb, vb, rel_pos_bias[i], rows).reshape(b, s_len, B_WIDTH)

        mix = jnp.concatenate([out_a, out_b], axis=-1) @ w_o[i]
        h = h + rms_norm(mix, post_mix_norm[i])

        m = rms_norm(h, pre_mlp_norm[i]) @ w_up[i]
        m = jnp.square(jax.nn.relu(m)) @ w_down[i]
        h = h + rms_norm(m, post_mlp_norm[i])

        gate = jax.nn.sigmoid(rms_norm(h, pre_ple_norm[i]) @ w_ple_gate[i])
        e = (p[i] @ w_ple_proj[i]) * gate
        h = h + rms_norm(e, post_ple_norm[i])
    return h
```

```python
import functools
import math

import jax
import jax.numpy as jnp
from jax import lax
from jax.experimental import pallas as pl
from jax.experimental.pallas import tpu as pltpu

F32 = jnp.float32
BF16 = jnp.bfloat16

HEAD_DIM = 128
GRID_W = 64
A_Q_HEADS = 8
A_KV_HEADS = 2
A_GROUP = A_Q_HEADS // A_KV_HEADS
B_HEADS = 8
WIN_ROWS = 8
WIN_COLS = 16
ROPE_THETA = 10000.0
NORM_EPS = 1e-6
LOG2E = math.log2(math.e)
SM_SCALE = 1.0 / math.sqrt(HEAD_DIM)
NEG_BIG = -1e30

QA_COL, KA_COL, VA_COL = 0, 8, 10
QB_COL, KB_COL, VB_COL = 12, 20, 28
A_COLS = 12 * HEAD_DIM

V7X_VMEM_BYTES = 64 * 1024 * 1024
VMEM_LIMIT = V7X_VMEM_BYTES - 8 * 1024 * 1024

NB_QROWS = 4
NB_KROWS = NB_QROWS + WIN_ROWS
NB_TQ = NB_QROWS * GRID_W
NB_TK = NB_KROWS * GRID_W


def _rms(x, g):
    return x * lax.rsqrt(jnp.mean(x * x, axis=-1, keepdims=True) + NORM_EPS) * g


def _inproj_kernel(x_ref, g_ref, w_ref, cos_ref, sin_ref, qn_ref, kn_ref, o_ref, xn_ref):
    j = pl.program_id(1)

    @pl.when(j == 0)
    def _():
        xn_ref[...] = _rms(x_ref[...], g_ref[...]).astype(BF16)

    acc = jnp.dot(xn_ref[...], w_ref[...], preferred_element_type=F32)

    @pl.when(j == 0)
    def _():
        cos = cos_ref[...]
        sin = sin_ref[...]
        lane = lax.broadcasted_iota(jnp.int32, cos.shape, 1)
        first_half = (lane % 64) < 32
        for hh in range(A_COLS // HEAD_DIM):
            blk = acc[:, hh * HEAD_DIM:(hh + 1) * HEAD_DIM]
            if hh < VA_COL:
                gain = qn_ref[...] if hh < KA_COL else kn_ref[...]
                y = _rms(blk, gain)
                partner = jnp.where(first_half, pltpu.roll(y, 96, 1), pltpu.roll(y, 32, 1))
                blk = y * cos + partner * sin
            o_ref[:, hh * HEAD_DIM:(hh + 1) * HEAD_DIM] = blk.astype(BF16)

    @pl.when(j > 0)
    def _():
        o_ref[...] = acc.astype(BF16)


def _inproj(x, g, w, cos, sin, qn, kn, *, tm=1024):
    s_len, d = x.shape
    n = w.shape[1]
    tn = A_COLS
    assert s_len % tm == 0 and n % tn == 0
    return pl.pallas_call(
        _inproj_kernel,
        grid=(s_len // tm, n // tn),
        in_specs=[
            pl.BlockSpec((tm, d), lambda i, j: (i, 0)),
            pl.BlockSpec((1, d), lambda i, j: (0, 0)),
            pl.BlockSpec((d, tn), lambda i, j: (0, j)),
            pl.BlockSpec((tm, HEAD_DIM), lambda i, j: (i, 0)),
            pl.BlockSpec((tm, HEAD_DIM), lambda i, j: (i, 0)),
            pl.BlockSpec((1, HEAD_DIM), lambda i, j: (0, 0)),
            pl.BlockSpec((1, HEAD_DIM), lambda i, j: (0, 0)),
        ],
        out_specs=pl.BlockSpec((tm, tn), lambda i, j: (i, j)),
        out_shape=jax.ShapeDtypeStruct((s_len, n), BF16),
        scratch_shapes=[pltpu.VMEM((tm, d), BF16)],
        compiler_params=pltpu.CompilerParams(
            dimension_semantics=("parallel", "arbitrary"), vmem_limit_bytes=VMEM_LIMIT),
        name="inproj",
    )(x, g, w, cos, sin, qn, kn)


def _attn_a_kernel(q_ref, k_ref, v_ref, o_ref, qs_ref, m_ref, l_ref, acc_ref, *, tq, tk, nk):
    for h in range(A_GROUP):
        qs_ref[h * tq:(h + 1) * tq, :] = q_ref[:, h * HEAD_DIM:(h + 1) * HEAD_DIM]
    m_ref[...] = jnp.full(m_ref.shape, -jnp.inf, F32)
    l_ref[...] = jnp.zeros(l_ref.shape, F32)
    acc_ref[...] = jnp.zeros(acc_ref.shape, F32)

    def body(j, carry):
        off = pl.multiple_of(j * tk, tk)
        kj = k_ref[pl.ds(off, tk), :]
        vj = v_ref[pl.ds(off, tk), :]
        s = lax.dot_general(qs_ref[...], kj, (((1,), (1,)), ((), ())),
                            preferred_element_type=F32)
        m_prev = m_ref[...]
        m_new = jnp.maximum(m_prev, jnp.max(s, axis=1, keepdims=True))
        p = jnp.exp2(s - m_new)
        alpha = jnp.exp2(m_prev - m_new)
        l_ref[...] = alpha * l_ref[...] + jnp.sum(p, axis=1, keepdims=True)
        acc_ref[...] = alpha * acc_ref[...] + jnp.dot(
            p.astype(BF16), vj, preferred_element_type=F32)
        m_ref[...] = m_new
        return carry

    lax.fori_loop(0, nk, body, 0)
    inv = 1.0 / l_ref[...]
    for h in range(A_GROUP):
        o_ref[:, h * HEAD_DIM:(h + 1) * HEAD_DIM] = (
            acc_ref[h * tq:(h + 1) * tq, :] * inv[h * tq:(h + 1) * tq, :]).astype(BF16)


def _attn_a(proj, *, tq=256, tk=512):
    s_len = proj.shape[0]
    assert s_len % tq == 0 and s_len % tk == 0
    m_rows = A_GROUP * tq
    gw = A_GROUP * HEAD_DIM
    kern = functools.partial(_attn_a_kernel, tq=tq, tk=tk, nk=s_len // tk)
    return pl.pallas_call(
        kern,
        grid=(A_KV_HEADS, s_len // tq),
        in_specs=[
            pl.BlockSpec((tq, gw), lambda g, i: (i, g)),
            pl.BlockSpec((s_len, HEAD_DIM), lambda g, i: (0, KA_COL + g)),
            pl.BlockSpec((s_len, HEAD_DIM), lambda g, i: (0, VA_COL + g)),
        ],
        out_specs=pl.BlockSpec((tq, gw), lambda g, i: (i, g)),
        out_shape=jax.ShapeDtypeStruct((s_len, A_Q_HEADS * HEAD_DIM), BF16),
        scratch_shapes=[
            pltpu.VMEM((m_rows, HEAD_DIM), BF16),
            pltpu.VMEM((m_rows, 1), F32),
            pltpu.VMEM((m_rows, 1), F32),
            pltpu.VMEM((m_rows, HEAD_DIM), F32),
        ],
        compiler_params=pltpu.CompilerParams(
            dimension_semantics=("parallel", "arbitrary"), vmem_limit_bytes=VMEM_LIMIT),
        name="attn_global",
    )(proj, proj, proj)


def _nb_block_types(rows):
    last_q = rows - NB_QROWS
    return ((0, 0), (2 * NB_QROWS, 2 * NB_QROWS - WIN_ROWS // 2), (last_q, rows - NB_KROWS))


def _attn_b_kernel(rpb_ref, q_ref, k_ref, v_ref, o_ref, tcol_ref, tab_ref, *, rows):
    h = pl.program_id(0)
    pb = pl.program_id(1)
    nblk = rows // NB_QROWS
    n_dr = 2 * WIN_ROWS - 1
    n_dc = 2 * WIN_COLS - 1

    @pl.when(pb == 0)
    def _build_tables():
        c = lax.broadcasted_iota(jnp.int32, (GRID_W, 2 * GRID_W), 0)
        kc = lax.broadcasted_iota(jnp.int32, (GRID_W, 2 * GRID_W), 1) % GRID_W
        c0 = jnp.clip(c - WIN_COLS // 2, 0, GRID_W - WIN_COLS)
        col_ok = (kc >= c0) & (kc < c0 + WIN_COLS)
        dc = kc - c + (WIN_COLS - 1)
        base = h * (n_dr * n_dc)

        def row_body(a, carry):
            t = jnp.full((GRID_W, 2 * GRID_W), NEG_BIG, F32)
            for b in range(n_dc):
                t = jnp.where(dc == b, rpb_ref[base + a * n_dc + b] * LOG2E, t)
            tcol_ref[a] = jnp.where(col_ok, t, NEG_BIG)
            return carry

        lax.fori_loop(0, n_dr, row_body, 0)

        lane = lax.broadcasted_iota(jnp.int32, (GRID_W, 2 * GRID_W), 1)
        neg = jnp.full((GRID_W, 2 * GRID_W), NEG_BIG, F32)
        for t, (r_first, k_first) in enumerate(_nb_block_types(rows)):
            for qr in range(NB_QROWS):
                r = r_first + qr
                r0 = min(max(r - WIN_ROWS // 2, 0), rows - WIN_ROWS)
                for jj in range(NB_KROWS // 2):
                    halves = []
                    for kr in (2 * jj, 2 * jj + 1):
                        k_abs = k_first + kr
                        ok = r0 <= k_abs < r0 + WIN_ROWS
                        halves.append(tcol_ref[k_abs - r + WIN_ROWS - 1] if ok else neg)
                    tab_ref[t, qr * GRID_W:(qr + 1) * GRID_W,
                            jj * 2 * GRID_W:(jj + 1) * 2 * GRID_W] = jnp.where(
                                lane < GRID_W, halves[0], halves[1])

    btype = jnp.where(pb == 0, 0, jnp.where(pb == nblk - 1, 2, 1))
    k_first = jnp.clip(pb * NB_QROWS - WIN_ROWS // 2, 0, rows - NB_KROWS)
    off = pl.multiple_of(k_first * GRID_W, GRID_W)
    ks = k_ref[pl.ds(off, NB_TK), :]
    vs = v_ref[pl.ds(off, NB_TK), :]
    s = lax.dot_general(q_ref[...], ks, (((1,), (1,)), ((), ())), preferred_element_type=F32)
    s = s * (SM_SCALE * LOG2E) + tab_ref[btype]
    m = jnp.max(s, axis=1, keepdims=True)
    p = jnp.exp2(s - m)
    l = jnp.sum(p, axis=1, keepdims=True)
    o = jnp.dot(p.astype(BF16), vs, preferred_element_type=F32)
    o_ref[...] = (o / l).astype(BF16)


def _attn_b(proj, rpb_flat):
    s_len = proj.shape[0]
    rows = s_len // GRID_W
    assert rows % NB_QROWS == 0 and rows >= 3 * NB_QROWS + WIN_ROWS
    kern = functools.partial(_attn_b_kernel, rows=rows)
    return pl.pallas_call(
        kern,
        grid=(B_HEADS, rows // NB_QROWS),
        in_specs=[
            pl.BlockSpec(memory_space=pltpu.SMEM),
            pl.BlockSpec((NB_TQ, HEAD_DIM), lambda h, p: (p, QB_COL + h)),
            pl.BlockSpec((s_len, HEAD_DIM), lambda h, p: (0, KB_COL + h)),
            pl.BlockSpec((s_len, HEAD_DIM), lambda h, p: (0, VB_COL + h)),
        ],
        out_specs=pl.BlockSpec((NB_TQ, HEAD_DIM), lambda h, p: (p, h)),
        out_shape=jax.ShapeDtypeStruct((s_len, B_HEADS * HEAD_DIM), BF16),
        scratch_shapes=[
            pltpu.VMEM((2 * WIN_ROWS - 1, GRID_W, 2 * GRID_W), F32),
            pltpu.VMEM((3, NB_TQ, NB_TK), F32),
        ],
        compiler_params=pltpu.CompilerParams(
            dimension_semantics=("parallel", "arbitrary"), vmem_limit_bytes=VMEM_LIMIT),
        name="attn_nbr",
    )(rpb_flat, proj, proj, proj)


def _oproj_kernel(oa_ref, ob_ref, wo_ref, x_ref, g_ref, h_ref):
    na = oa_ref.shape[1]
    mix = jnp.dot(oa_ref[...], wo_ref[:na, :], preferred_element_type=F32)
    mix = mix + jnp.dot(ob_ref[...], wo_ref[na:, :], preferred_element_type=F32)
    h_ref[...] = x_ref[...] + _rms(mix, g_ref[...])


def _oproj(oa, ob, wo, x, g, *, tm=512):
    s_len, d = x.shape
    assert s_len % tm == 0
    return pl.pallas_call(
        _oproj_kernel,
        grid=(s_len // tm,),
        in_specs=[
            pl.BlockSpec((tm, oa.shape[1]), lambda i: (i, 0)),
            pl.BlockSpec((tm, ob.shape[1]), lambda i: (i, 0)),
            pl.BlockSpec(wo.shape, lambda i: (0, 0)),
            pl.BlockSpec((tm, d), lambda i: (i, 0)),
            pl.BlockSpec((1, d), lambda i: (0, 0)),
        ],
        out_specs=pl.BlockSpec((tm, d), lambda i: (i, 0)),
        out_shape=jax.ShapeDtypeStruct((s_len, d), F32),
        compiler_params=pltpu.CompilerParams(
            dimension_semantics=("parallel",), vmem_limit_bytes=VMEM_LIMIT),
        name="oproj",
    )(oa, ob, wo, x, g)


def _mlp_kernel(h_ref, gpre_ref, wup_ref, wdn_ref, gpost_ref, o_ref, xn_ref, acc_ref):
    f = pl.program_id(1)

    @pl.when(f == 0)
    def _():
        xn_ref[...] = _rms(h_ref[...], gpre_ref[...]).astype(BF16)
        acc_ref[...] = jnp.zeros(acc_ref.shape, F32)

    u = jnp.dot(xn_ref[...], wup_ref[...], preferred_element_type=F32)
    a = jnp.square(jnp.maximum(u, 0.0)).astype(BF16)
    acc_ref[...] += jnp.dot(a, wdn_ref[...], preferred_element_type=F32)

    @pl.when(f == pl.num_programs(1) - 1)
    def _():
        o_ref[...] = h_ref[...] + _rms(acc_ref[...], gpost_ref[...])


def _mlp(h, gpre, wup, wdn, gpost, *, tm=512, tf=1024):
    s_len, d = h.shape
    d_ff = wup.shape[1]
    assert s_len % tm == 0 and d_ff % tf == 0
    return pl.pallas_call(
        _mlp_kernel,
        grid=(s_len // tm, d_ff // tf),
        in_specs=[
            pl.BlockSpec((tm, d), lambda i, f: (i, 0)),
            pl.BlockSpec((1, d), lambda i, f: (0, 0)),
            pl.BlockSpec((d, tf), lambda i, f: (0, f)),
            pl.BlockSpec((tf, d), lambda i, f: (f, 0)),
            pl.BlockSpec((1, d), lambda i, f: (0, 0)),
        ],
        out_specs=pl.BlockSpec((tm, d), lambda i, f: (i, 0)),
        out_shape=jax.ShapeDtypeStruct((s_len, d), F32),
        scratch_shapes=[pltpu.VMEM((tm, d), BF16), pltpu.VMEM((tm, d), F32)],
        compiler_params=pltpu.CompilerParams(
            dimension_semantics=("parallel", "arbitrary"), vmem_limit_bytes=VMEM_LIMIT),
        name="mlp",
    )(h, gpre, wup, wdn, gpost)


def _ple_kernel(h_ref, p_ref, gpre_ref, wg_ref, wp_ref, gpost_ref, o_ref):
    h = h_ref[...]
    xn = _rms(h, gpre_ref[...]).astype(BF16)
    gate = jax.nn.sigmoid(jnp.dot(xn, wg_ref[...], preferred_element_type=F32))
    e = jnp.dot(p_ref[...].astype(BF16), wp_ref[...], preferred_element_type=F32) * gate
    o_ref[...] = h + _rms(e, gpost_ref[...])


def _ple(h, p, gpre, wg, wp, gpost, *, tm=512):
    s_len, d = h.shape
    dp = p.shape[1]
    assert s_len % tm == 0
    return pl.pallas_call(
        _ple_kernel,
        grid=(s_len // tm,),
        in_specs=[
            pl.BlockSpec((tm, d), lambda i: (i, 0)),
            pl.BlockSpec((tm, dp), lambda i: (i, 0)),
            pl.BlockSpec((1, d), lambda i: (0, 0)),
            pl.BlockSpec(wg.shape, lambda i: (0, 0)),
            pl.BlockSpec(wp.shape, lambda i: (0, 0)),
            pl.BlockSpec((1, d), lambda i: (0, 0)),
        ],
        out_specs=pl.BlockSpec((tm, d), lambda i: (i, 0)),
        out_shape=jax.ShapeDtypeStruct((s_len, d), F32),
        compiler_params=pltpu.CompilerParams(
            dimension_semantics=("parallel",), vmem_limit_bytes=VMEM_LIMIT),
        name="ple",
    )(h, p, gpre, wg, wp, gpost)


def _rope_tables(s_len):
    t = jnp.arange(s_len, dtype=jnp.int32)
    row = (t // GRID_W).astype(F32)
    col = (t % GRID_W).astype(F32)
    n_freq = HEAD_DIM // 4
    freqs = ROPE_THETA ** (-jnp.arange(n_freq, dtype=F32) / n_freq)
    ar = row[:, None] * freqs[None, :]
    ac = col[:, None] * freqs[None, :]
    cos = jnp.concatenate([jnp.cos(ar), jnp.cos(ar), jnp.cos(ac), jnp.cos(ac)], axis=-1)
    sin = jnp.concatenate([-jnp.sin(ar), jnp.sin(ar), -jnp.sin(ac), jnp.sin(ac)], axis=-1)
    return cos, sin


def kernel(x, p, pre_mix_norm, w_in, q_norm, k_norm, rel_pos_bias, w_o, post_mix_norm,
           pre_mlp_norm, w_up, w_down, post_mlp_norm, pre_ple_norm, w_ple_gate, w_ple_proj,
           post_ple_norm):
    b, s_len, d = x.shape
    depth = w_in.shape[0]
    cos, sin = _rope_tables(s_len)
    outs = []
    for bi in range(b):
        h = x[bi]
        for i in range(depth):
            qn = (q_norm[i] * (SM_SCALE * LOG2E)).reshape(1, HEAD_DIM)
            kn = k_norm[i].reshape(1, HEAD_DIM)
            proj = _inproj(h, pre_mix_norm[i].reshape(1, d), w_in[i].astype(BF16), cos, sin, qn, kn)
            out_a = _attn_a(proj)
            out_b = _attn_b(proj, rel_pos_bias[i].reshape(-1))
            h = _oproj(out_a, out_b, w_o[i].astype(BF16), h, post_mix_norm[i].reshape(1, d))
            h = _mlp(h, pre_mlp_norm[i].reshape(1, d), w_up[i].astype(BF16),
                     w_down[i].astype(BF16), post_mlp_norm[i].reshape(1, d))
            h = _ple(h, p[i, bi], pre_ple_norm[i].reshape(1, d), w_ple_gate[i].astype(BF16),
                     w_ple_proj[i].astype(BF16), post_ple_norm[i].reshape(1, d))
        outs.append(h)
    return jnp.stack(outs, axis=0)
```

```python
import functools
import math

import jax
import jax.numpy as jnp
from jax import lax
from jax.experimental import pallas as pl
from jax.experimental.pallas import tpu as pltpu

F32 = jnp.float32
BF16 = jnp.bfloat16

HEAD_DIM = 128
GRID_W = 64
A_Q_HEADS = 8
A_KV_HEADS = 2
A_GROUP = A_Q_HEADS // A_KV_HEADS
B_HEADS = 8
WIN_ROWS = 8
WIN_COLS = 16
ROPE_THETA = 10000.0
NORM_EPS = 1e-6
LOG2E = math.log2(math.e)
SM_SCALE = 1.0 / math.sqrt(HEAD_DIM)
NEG_BIG = -1e30

QA_COL, KA_COL, VA_COL = 0, 8, 10
QB_COL, KB_COL, VB_COL = 12, 20, 28
A_COLS = 12 * HEAD_DIM

V7X_VMEM_BYTES = 64 * 1024 * 1024
VMEM_LIMIT = V7X_VMEM_BYTES - 8 * 1024 * 1024

NB_QROWS = 4
NB_KROWS = NB_QROWS + WIN_ROWS
NB_TQ = NB_QROWS * GRID_W
NB_TK = NB_KROWS * GRID_W


def _rms(x, g):
    return x * lax.rsqrt(jnp.mean(x * x, axis=-1, keepdims=True) + NORM_EPS) * g


def _inproj_kernel(x_ref, g_ref, w_ref, cos_ref, sin_ref, qn_ref, kn_ref, o_ref, xn_ref):
    j = pl.program_id(1)

    @pl.when(j == 0)
    def _():
        xn_ref[...] = _rms(x_ref[...], g_ref[...]).astype(BF16)

    acc = jnp.dot(xn_ref[...], w_ref[...], preferred_element_type=F32)

    @pl.when(j == 0)
    def _():
        cos = cos_ref[...]
        sin = sin_ref[...]
        lane = lax.broadcasted_iota(jnp.int32, cos.shape, 1)
        first_half = (lane % 64) < 32
        for hh in range(A_COLS // HEAD_DIM):
            blk = acc[:, hh * HEAD_DIM:(hh + 1) * HEAD_DIM]
            if hh < VA_COL:
                gain = qn_ref[...] if hh < KA_COL else kn_ref[...]
                y = _rms(blk, gain)
                partner = jnp.where(first_half, pltpu.roll(y, 96, 1), pltpu.roll(y, 32, 1))
                blk = y * cos + partner * sin
            o_ref[:, hh * HEAD_DIM:(hh + 1) * HEAD_DIM] = blk.astype(BF16)

    @pl.when(j > 0)
    def _():
        o_ref[...] = acc.astype(BF16)


def _inproj(x, g, w, cos, sin, qn, kn, *, tm=1024):
    s_len, d = x.shape
    n = w.shape[1]
    tn = A_COLS
    assert s_len % tm == 0 and n % tn == 0
    return pl.pallas_call(
        _inproj_kernel,
        grid=(s_len // tm, n // tn),
        in_specs=[
            pl.BlockSpec((tm, d), lambda i, j: (i, 0)),
            pl.BlockSpec((1, d), lambda i, j: (0, 0)),
            pl.BlockSpec((d, tn), lambda i, j: (0, j)),
            pl.BlockSpec((tm, HEAD_DIM), lambda i, j: (i, 0)),
            pl.BlockSpec((tm, HEAD_DIM), lambda i, j: (i, 0)),
            pl.BlockSpec((1, HEAD_DIM), lambda i, j: (0, 0)),
            pl.BlockSpec((1, HEAD_DIM), lambda i, j: (0, 0)),
        ],
        out_specs=pl.BlockSpec((tm, tn), lambda i, j: (i, j)),
        out_shape=jax.ShapeDtypeStruct((s_len, n), BF16),
        scratch_shapes=[pltpu.VMEM((tm, d), BF16)],
        compiler_params=pltpu.CompilerParams(
            dimension_semantics=("parallel", "arbitrary"), vmem_limit_bytes=VMEM_LIMIT),
        name="inproj",
    )(x, g, w, cos, sin, qn, kn)


EXP2_SAFE_BOUND = 60.0


def _attn_a_kernel(q_ref, k_ref, v_ref, o_ref, qs_ref, kt_ref, vx_ref, kmax_ref, m_ref, acc_ref,
                   *, tq, tk, nk):
    i = pl.program_id(1)

    @pl.when(i == 0)
    def _prepare_kv():
        lane = lax.broadcasted_iota(jnp.int32, (tk, HEAD_DIM), 1)
        ones_col = jnp.where(lane == 0, 1.0, 0.0).astype(BF16)

        def prep(c, kmax):
            off = pl.multiple_of(c * tk, tk)
            ktf = k_ref[pl.ds(off, tk), :].astype(F32).T
            kt_ref[c] = ktf.astype(BF16)
            vx_ref[pl.ds(off, tk), :HEAD_DIM] = v_ref[pl.ds(off, tk), :]
            vx_ref[pl.ds(off, tk), HEAD_DIM:] = ones_col
            return jnp.maximum(kmax, jnp.sum(ktf * ktf, axis=0, keepdims=True))

        kmax = lax.fori_loop(0, nk, prep, jnp.zeros((1, tk), F32))
        kmax_ref[0] = jnp.max(kmax)

    for h in range(A_GROUP):
        qs_ref[h * tq:(h + 1) * tq, :] = q_ref[:, h * HEAD_DIM:(h + 1) * HEAD_DIM]
    acc_ref[...] = jnp.zeros(acc_ref.shape, F32)
    qf = qs_ref[...].astype(F32)
    qmax = jnp.max(jnp.sum(qf * qf, axis=1, keepdims=True))
    no_max_needed = qmax * kmax_ref[0] <= EXP2_SAFE_BOUND * EXP2_SAFE_BOUND

    @pl.when(no_max_needed)
    def _plain():
        def body(j, carry):
            off = pl.multiple_of(j * tk, tk)
            s = jnp.dot(qs_ref[...], kt_ref[j], preferred_element_type=F32)
            p = jnp.exp2(s).astype(BF16)
            acc_ref[...] += jnp.dot(p, vx_ref[pl.ds(off, tk), :], preferred_element_type=F32)
            return carry

        lax.fori_loop(0, nk, body, 0)

    @pl.when(jnp.logical_not(no_max_needed))
    def _online():
        m_ref[...] = jnp.full(m_ref.shape, -jnp.inf, F32)

        def body(j, carry):
            off = pl.multiple_of(j * tk, tk)
            s = jnp.dot(qs_ref[...], kt_ref[j], preferred_element_type=F32)
            m_prev = m_ref[...]
            m_new = jnp.maximum(m_prev, jnp.max(s, axis=1, keepdims=True))
            p = jnp.exp2(s - m_new).astype(BF16)
            acc_ref[...] = jnp.exp2(m_prev - m_new) * acc_ref[...] + jnp.dot(
                p, vx_ref[pl.ds(off, tk), :], preferred_element_type=F32)
            m_ref[...] = m_new
            return carry

        lax.fori_loop(0, nk, body, 0)

    for h in range(A_GROUP):
        a = acc_ref[h * tq:(h + 1) * tq, :]
        o_ref[:, h * HEAD_DIM:(h + 1) * HEAD_DIM] = (
            a[:, :HEAD_DIM] / a[:, HEAD_DIM:HEAD_DIM + 1]).astype(BF16)


def _attn_a(proj, *, tq=256, tk=512):
    s_len = proj.shape[0]
    assert s_len % tq == 0 and s_len % tk == 0
    m_rows = A_GROUP * tq
    gw = A_GROUP * HEAD_DIM
    nk = s_len // tk
    kern = functools.partial(_attn_a_kernel, tq=tq, tk=tk, nk=nk)
    return pl.pallas_call(
        kern,
        grid=(A_KV_HEADS, s_len // tq),
        in_specs=[
            pl.BlockSpec((tq, gw), lambda g, i: (i, g)),
            pl.BlockSpec((s_len, HEAD_DIM), lambda g, i: (0, KA_COL + g)),
            pl.BlockSpec((s_len, HEAD_DIM), lambda g, i: (0, VA_COL + g)),
        ],
        out_specs=pl.BlockSpec((tq, gw), lambda g, i: (i, g)),
        out_shape=jax.ShapeDtypeStruct((s_len, A_Q_HEADS * HEAD_DIM), BF16),
        scratch_shapes=[
            pltpu.VMEM((m_rows, HEAD_DIM), BF16),
            pltpu.VMEM((nk, HEAD_DIM, tk), BF16),
            pltpu.VMEM((s_len, 2 * HEAD_DIM), BF16),
            pltpu.SMEM((1,), F32),
            pltpu.VMEM((m_rows, 1), F32),
            pltpu.VMEM((m_rows, 2 * HEAD_DIM), F32),
        ],
        compiler_params=pltpu.CompilerParams(
            dimension_semantics=("arbitrary", "arbitrary"), vmem_limit_bytes=VMEM_LIMIT),
        name="attn_global",
    )(proj, proj, proj)


def _nb_block_types(rows):
    last_q = rows - NB_QROWS
    return ((0, 0), (2 * NB_QROWS, 2 * NB_QROWS - WIN_ROWS // 2), (last_q, rows - NB_KROWS))


def _attn_b_kernel(rpb_ref, q_ref, k_ref, v_ref, o_ref, tcol_ref, tab_ref, *, rows):
    h = pl.program_id(0)
    pb = pl.program_id(1)
    nblk = rows // NB_QROWS
    n_dr = 2 * WIN_ROWS - 1
    n_dc = 2 * WIN_COLS - 1

    @pl.when(pb == 0)
    def _build_tables():
        c = lax.broadcasted_iota(jnp.int32, (GRID_W, 2 * GRID_W), 0)
        kc = lax.broadcasted_iota(jnp.int32, (GRID_W, 2 * GRID_W), 1) % GRID_W
        c0 = jnp.clip(c - WIN_COLS // 2, 0, GRID_W - WIN_COLS)
        col_ok = (kc >= c0) & (kc < c0 + WIN_COLS)
        dc = kc - c + (WIN_COLS - 1)
        base = h * (n_dr * n_dc)

        def row_body(a, carry):
            t = jnp.full((GRID_W, 2 * GRID_W), NEG_BIG, F32)
            for b in range(n_dc):
                t = jnp.where(dc == b, rpb_ref[base + a * n_dc + b] * LOG2E, t)
            tcol_ref[a] = jnp.where(col_ok, t, NEG_BIG)
            return carry

        lax.fori_loop(0, n_dr, row_body, 0)

        lane = lax.broadcasted_iota(jnp.int32, (GRID_W, 2 * GRID_W), 1)
        neg = jnp.full((GRID_W, 2 * GRID_W), NEG_BIG, F32)
        for t, (r_first, k_first) in enumerate(_nb_block_types(rows)):
            for qr in range(NB_QROWS):
                r = r_first + qr
                r0 = min(max(r - WIN_ROWS // 2, 0), rows - WIN_ROWS)
                for jj in range(NB_KROWS // 2):
                    halves = []
                    for kr in (2 * jj, 2 * jj + 1):
                        k_abs = k_first + kr
                        ok = r0 <= k_abs < r0 + WIN_ROWS
                        halves.append(tcol_ref[k_abs - r + WIN_ROWS - 1] if ok else neg)
                    tab_ref[t, qr * GRID_W:(qr + 1) * GRID_W,
                            jj * 2 * GRID_W:(jj + 1) * 2 * GRID_W] = jnp.where(
                                lane < GRID_W, halves[0], halves[1])

    btype = jnp.where(pb == 0, 0, jnp.where(pb == nblk - 1, 2, 1))
    k_first = jnp.clip(pb * NB_QROWS - WIN_ROWS // 2, 0, rows - NB_KROWS)
    off = pl.multiple_of(k_first * GRID_W, GRID_W)
    ks = k_ref[pl.ds(off, NB_TK), :]
    vs = v_ref[pl.ds(off, NB_TK), :]
    s = lax.dot_general(q_ref[...], ks, (((1,), (1,)), ((), ())), preferred_element_type=F32)
    s = s * (SM_SCALE * LOG2E) + tab_ref[btype]
    m = jnp.max(s, axis=1, keepdims=True)
    p = jnp.exp2(s - m)
    l = jnp.sum(p, axis=1, keepdims=True)
    o = jnp.dot(p.astype(BF16), vs, preferred_element_type=F32)
    o_ref[...] = (o / l).astype(BF16)


def _attn_b(proj, rpb_flat):
    s_len = proj.shape[0]
    rows = s_len // GRID_W
    assert rows % NB_QROWS == 0 and rows >= 3 * NB_QROWS + WIN_ROWS
    kern = functools.partial(_attn_b_kernel, rows=rows)
    return pl.pallas_call(
        kern,
        grid=(B_HEADS, rows // NB_QROWS),
        in_specs=[
            pl.BlockSpec(memory_space=pltpu.SMEM),
            pl.BlockSpec((NB_TQ, HEAD_DIM), lambda h, p: (p, QB_COL + h)),
            pl.BlockSpec((s_len, HEAD_DIM), lambda h, p: (0, KB_COL + h)),
            pl.BlockSpec((s_len, HEAD_DIM), lambda h, p: (0, VB_COL + h)),
        ],
        out_specs=pl.BlockSpec((NB_TQ, HEAD_DIM), lambda h, p: (p, h)),
        out_shape=jax.ShapeDtypeStruct((s_len, B_HEADS * HEAD_DIM), BF16),
        scratch_shapes=[
            pltpu.VMEM((2 * WIN_ROWS - 1, GRID_W, 2 * GRID_W), F32),
            pltpu.VMEM((3, NB_TQ, NB_TK), F32),
        ],
        compiler_params=pltpu.CompilerParams(
            dimension_semantics=("parallel", "arbitrary"), vmem_limit_bytes=VMEM_LIMIT),
        name="attn_nbr",
    )(rpb_flat, proj, proj, proj)


def _oproj_kernel(oa_ref, ob_ref, wo_ref, x_ref, g_ref, h_ref):
    na = oa_ref.shape[1]
    mix = jnp.dot(oa_ref[...], wo_ref[:na, :], preferred_element_type=F32)
    mix = mix + jnp.dot(ob_ref[...], wo_ref[na:, :], preferred_element_type=F32)
    h_ref[...] = x_ref[...] + _rms(mix, g_ref[...])


def _oproj(oa, ob, wo, x, g, *, tm=512):
    s_len, d = x.shape
    assert s_len % tm == 0
    return pl.pallas_call(
        _oproj_kernel,
        grid=(s_len // tm,),
        in_specs=[
            pl.BlockSpec((tm, oa.shape[1]), lambda i: (i, 0)),
            pl.BlockSpec((tm, ob.shape[1]), lambda i: (i, 0)),
            pl.BlockSpec(wo.shape, lambda i: (0, 0)),
            pl.BlockSpec((tm, d), lambda i: (i, 0)),
            pl.BlockSpec((1, d), lambda i: (0, 0)),
        ],
        out_specs=pl.BlockSpec((tm, d), lambda i: (i, 0)),
        out_shape=jax.ShapeDtypeStruct((s_len, d), F32),
        compiler_params=pltpu.CompilerParams(
            dimension_semantics=("parallel",), vmem_limit_bytes=VMEM_LIMIT),
        name="oproj",
    )(oa, ob, wo, x, g)


def _mlp_kernel(h_ref, gpre_ref, wup_ref, wdn_ref, gpost_ref, o_ref, xn_ref, acc_ref):
    f = pl.program_id(1)

    @pl.when(f == 0)
    def _():
        xn_ref[...] = _rms(h_ref[...], gpre_ref[...]).astype(BF16)
        acc_ref[...] = jnp.zeros(acc_ref.shape, F32)

    u = jnp.dot(xn_ref[...], wup_ref[...], preferred_element_type=F32)
    a = jnp.square(jnp.maximum(u, 0.0)).astype(BF16)
    acc_ref[...] += jnp.dot(a, wdn_ref[...], preferred_element_type=F32)

    @pl.when(f == pl.num_programs(1) - 1)
    def _():
        o_ref[...] = h_ref[...] + _rms(acc_ref[...], gpost_ref[...])


def _mlp(h, gpre, wup, wdn, gpost, *, tm=512, tf=1024):
    s_len, d = h.shape
    d_ff = wup.shape[1]
    assert s_len % tm == 0 and d_ff % tf == 0
    return pl.pallas_call(
        _mlp_kernel,
        grid=(s_len // tm, d_ff // tf),
        in_specs=[
            pl.BlockSpec((tm, d), lambda i, f: (i, 0)),
            pl.BlockSpec((1, d), lambda i, f: (0, 0)),
            pl.BlockSpec((d, tf), lambda i, f: (0, f)),
            pl.BlockSpec((tf, d), lambda i, f: (f, 0)),
            pl.BlockSpec((1, d), lambda i, f: (0, 0)),
        ],
        out_specs=pl.BlockSpec((tm, d), lambda i, f: (i, 0)),
        out_shape=jax.ShapeDtypeStruct((s_len, d), F32),
        scratch_shapes=[pltpu.VMEM((tm, d), BF16), pltpu.VMEM((tm, d), F32)],
        compiler_params=pltpu.CompilerParams(
            dimension_semantics=("parallel", "arbitrary"), vmem_limit_bytes=VMEM_LIMIT),
        name="mlp",
    )(h, gpre, wup, wdn, gpost)


def _ple_kernel(h_ref, p_ref, gpre_ref, wg_ref, wp_ref, gpost_ref, o_ref):
    h = h_ref[...]
    xn = _rms(h, gpre_ref[...]).astype(BF16)
    gate = jax.nn.sigmoid(jnp.dot(xn, wg_ref[...], preferred_element_type=F32))
    e = jnp.dot(p_ref[...].astype(BF16), wp_ref[...], preferred_element_type=F32) * gate
    o_ref[...] = h + _rms(e, gpost_ref[...])


def _ple(h, p, gpre, wg, wp, gpost, *, tm=512):
    s_len, d = h.shape
    dp = p.shape[1]
    assert s_len % tm == 0
    return pl.pallas_call(
        _ple_kernel,
        grid=(s_len // tm,),
        in_specs=[
            pl.BlockSpec((tm, d), lambda i: (i, 0)),
            pl.BlockSpec((tm, dp), lambda i: (i, 0)),
            pl.BlockSpec((1, d), lambda i: (0, 0)),
            pl.BlockSpec(wg.shape, lambda i: (0, 0)),
            pl.BlockSpec(wp.shape, lambda i: (0, 0)),
            pl.BlockSpec((1, d), lambda i: (0, 0)),
        ],
        out_specs=pl.BlockSpec((tm, d), lambda i: (i, 0)),
        out_shape=jax.ShapeDtypeStruct((s_len, d), F32),
        compiler_params=pltpu.CompilerParams(
            dimension_semantics=("parallel",), vmem_limit_bytes=VMEM_LIMIT),
        name="ple",
    )(h, p, gpre, wg, wp, gpost)


def _rope_tables(s_len):
    t = jnp.arange(s_len, dtype=jnp.int32)
    row = (t // GRID_W).astype(F32)
    col = (t % GRID_W).astype(F32)
    n_freq = HEAD_DIM // 4
    freqs = ROPE_THETA ** (-jnp.arange(n_freq, dtype=F32) / n_freq)
    ar = row[:, None] * freqs[None, :]
    ac = col[:, None] * freqs[None, :]
    cos = jnp.concatenate([jnp.cos(ar), jnp.cos(ar), jnp.cos(ac), jnp.cos(ac)], axis=-1)
    sin = jnp.concatenate([-jnp.sin(ar), jnp.sin(ar), -jnp.sin(ac), jnp.sin(ac)], axis=-1)
    return cos, sin


def kernel(x, p, pre_mix_norm, w_in, q_norm, k_norm, rel_pos_bias, w_o, post_mix_norm,
           pre_mlp_norm, w_up, w_down, post_mlp_norm, pre_ple_norm, w_ple_gate, w_ple_proj,
           post_ple_norm):
    b, s_len, d = x.shape
    depth = w_in.shape[0]
    cos, sin = _rope_tables(s_len)
    outs = []
    for bi in range(b):
        h = x[bi]
        for i in range(depth):
            qn = (q_norm[i] * (SM_SCALE * LOG2E)).reshape(1, HEAD_DIM)
            kn = k_norm[i].reshape(1, HEAD_DIM)
            proj = _inproj(h, pre_mix_norm[i].reshape(1, d), w_in[i].astype(BF16), cos, sin, qn, kn)
            out_a = _attn_a(proj)
            out_b = _attn_b(proj, rel_pos_bias[i].reshape(-1))
            h = _oproj(out_a, out_b, w_o[i].astype(BF16), h, post_mix_norm[i].reshape(1, d))
            h = _mlp(h, pre_mlp_norm[i].reshape(1, d), w_up[i].astype(BF16),
                     w_down[i].astype(BF16), post_mlp_norm[i].reshape(1, d))
            h = _ple(h, p[i, bi], pre_ple_norm[i].reshape(1, d), w_ple_gate[i].astype(BF16),
                     w_ple_proj[i].astype(BF16), post_ple_norm[i].reshape(1, d))
        outs.append(h)
    return jnp.stack(outs, axis=0)
```

```python
import functools
import math

import jax
import jax.numpy as jnp
from jax import lax
from jax.experimental import pallas as pl
from jax.experimental.pallas import tpu as pltpu

F32 = jnp.float32
BF16 = jnp.bfloat16

HEAD_DIM = 128
GRID_W = 64
A_Q_HEADS = 8
A_KV_HEADS = 2
A_GROUP = A_Q_HEADS // A_KV_HEADS
B_HEADS = 8
WIN_ROWS = 8
WIN_COLS = 16
ROPE_THETA = 10000.0
NORM_EPS = 1e-6
LOG2E = math.log2(math.e)
SM_SCALE = 1.0 / math.sqrt(HEAD_DIM)
NEG_BIG = -1e30

QA_COL, KA_COL, VA_COL = 0, 8, 10
QB_COL, KB_COL, VB_COL = 12, 20, 28
A_COLS = 12 * HEAD_DIM

V7X_VMEM_BYTES = 64 * 1024 * 1024
VMEM_LIMIT = V7X_VMEM_BYTES - 8 * 1024 * 1024

NB_QROWS = 4
NB_KROWS = NB_QROWS + WIN_ROWS
NB_TQ = NB_QROWS * GRID_W
NB_TK = NB_KROWS * GRID_W


def _rms(x, g):
    return x * lax.rsqrt(jnp.mean(x * x, axis=-1, keepdims=True) + NORM_EPS) * g


def _lane_sum_mxu(v):
    hi = v.astype(BF16)
    lo = (v - hi.astype(F32)).astype(BF16)
    ones = jnp.ones((2 * v.shape[1], v.shape[1]), BF16)
    return jnp.dot(jnp.concatenate([hi, lo], axis=1), ones, preferred_element_type=F32)


def _inproj_kernel(x_ref, g_ref, w_ref, cos_ref, sin_ref, qn_ref, kn_ref, o_ref, xn_ref):
    j = pl.program_id(1)

    @pl.when(j == 0)
    def _():
        xn_ref[...] = _rms(x_ref[...], g_ref[...]).astype(BF16)

    @pl.when(j == 0)
    def _():
        cos = cos_ref[...]
        sin = sin_ref[...]
        hpd = 4
        for grp in range(A_COLS // (hpd * HEAD_DIM)):
            c0 = grp * hpd * HEAD_DIM
            acc = jnp.dot(xn_ref[...], w_ref[:, c0:c0 + hpd * HEAD_DIM],
                          preferred_element_type=F32)
            for sub in range(hpd):
                hh = hpd * grp + sub
                blk = acc[:, sub * HEAD_DIM:(sub + 1) * HEAD_DIM]
                if hh < VA_COL:
                    gain = qn_ref[...] if hh < KA_COL else kn_ref[...]
                    y = blk * lax.rsqrt(_lane_sum_mxu(blk * blk) * (1.0 / HEAD_DIM) + NORM_EPS) * gain
                    blk = y * cos + pltpu.roll(y, HEAD_DIM // 2, 1) * sin
                o_ref[:, hh * HEAD_DIM:(hh + 1) * HEAD_DIM] = blk.astype(BF16)

    @pl.when(j == 1)
    def _():
        nq = B_HEADS * HEAD_DIM
        acc = jnp.dot(xn_ref[...], w_ref[...], preferred_element_type=F32)
        o_ref[:, :nq] = (acc[:, :nq] * (SM_SCALE * LOG2E)).astype(BF16)
        o_ref[:, nq:] = acc[:, nq:].astype(BF16)

    @pl.when(j > 1)
    def _():
        o_ref[...] = jnp.dot(xn_ref[...], w_ref[...], preferred_element_type=F32).astype(BF16)


def _inproj(x, g, w, cos, sin, qn, kn, *, tm=1024):
    s_len, d = x.shape
    n = w.shape[1]
    tn = A_COLS
    assert s_len % tm == 0 and n % tn == 0
    return pl.pallas_call(
        _inproj_kernel,
        grid=(s_len // tm, n // tn),
        in_specs=[
            pl.BlockSpec((tm, d), lambda i, j: (i, 0)),
            pl.BlockSpec((1, d), lambda i, j: (0, 0)),
            pl.BlockSpec((d, tn), lambda i, j: (0, j)),
            pl.BlockSpec((tm, HEAD_DIM), lambda i, j: (i, 0)),
            pl.BlockSpec((tm, HEAD_DIM), lambda i, j: (i, 0)),
            pl.BlockSpec((1, HEAD_DIM), lambda i, j: (0, 0)),
            pl.BlockSpec((1, HEAD_DIM), lambda i, j: (0, 0)),
        ],
        out_specs=pl.BlockSpec((tm, tn), lambda i, j: (i, j)),
        out_shape=jax.ShapeDtypeStruct((s_len, n), BF16),
        scratch_shapes=[pltpu.VMEM((tm, d), BF16)],
        compiler_params=pltpu.CompilerParams(
            dimension_semantics=("parallel", "arbitrary"), vmem_limit_bytes=VMEM_LIMIT),
        name="inproj",
    )(x, g, w, cos, sin, qn, kn)


EXP2_SAFE_BOUND = 60.0


ONES_ROWS = 16


def _attn_a_kernel(q_ref, k_ref, v_ref, o_ref, qt_ref, vxt_ref, kmax_ref, m_ref, acc_ref,
                   *, tq, tk, nk):
    i = pl.program_id(1)

    @pl.when(i == 0)
    def _prepare_kv():
        row = lax.broadcasted_iota(jnp.int32, (ONES_ROWS, tk), 0)
        ones_rows = jnp.where(row == 0, 1.0, 0.0).astype(BF16)

        def prep(c, kmax):
            off = pl.multiple_of(c * tk, tk)
            vxt_ref[c, :HEAD_DIM, :] = v_ref[pl.ds(off, tk), :].astype(F32).T.astype(BF16)
            vxt_ref[c, HEAD_DIM:, :] = ones_rows
            kf = k_ref[pl.ds(off, tk), :].astype(F32)
            return jnp.maximum(kmax, jnp.sum(kf * kf, axis=1, keepdims=True))

        kmax = lax.fori_loop(0, nk, prep, jnp.zeros((tk, 1), F32))
        kmax_ref[0] = jnp.max(kmax)

    for h in range(A_GROUP):
        qt_ref[:, h * tq:(h + 1) * tq] = (
            q_ref[:, h * HEAD_DIM:(h + 1) * HEAD_DIM].astype(F32).T.astype(BF16))
    acc_ref[...] = jnp.zeros(acc_ref.shape, F32)
    qf = qt_ref[...].astype(F32)
    qmax = jnp.max(jnp.sum(qf * qf, axis=0, keepdims=True))
    no_max_needed = qmax * kmax_ref[0] <= EXP2_SAFE_BOUND * EXP2_SAFE_BOUND

    @pl.when(no_max_needed)
    def _plain():
        def body(j, carry):
            off = pl.multiple_of(j * tk, tk)
            s = jnp.dot(k_ref[pl.ds(off, tk), :], qt_ref[...], preferred_element_type=F32)
            p = jnp.exp2(s).astype(BF16)
            acc_ref[...] += jnp.dot(vxt_ref[j], p, preferred_element_type=F32)
            return carry

        lax.fori_loop(0, nk, body, 0, unroll=8)

    @pl.when(jnp.logical_not(no_max_needed))
    def _online():
        m_ref[...] = jnp.full(m_ref.shape, -jnp.inf, F32)

        def body(j, carry):
            off = pl.multiple_of(j * tk, tk)
            s = jnp.dot(k_ref[pl.ds(off, tk), :], qt_ref[...], preferred_element_type=F32)
            m_prev = m_ref[...]
            m_new = jnp.maximum(m_prev, jnp.max(s, axis=0, keepdims=True))
            p = jnp.exp2(s - m_new).astype(BF16)
            acc_ref[...] = jnp.exp2(m_prev - m_new) * acc_ref[...] + jnp.dot(
                vxt_ref[j], p, preferred_element_type=F32)
            m_ref[...] = m_new
            return carry

        lax.fori_loop(0, nk, body, 0)

    out_t = acc_ref[:HEAD_DIM, :] / acc_ref[HEAD_DIM:HEAD_DIM + 1, :]
    for h in range(A_GROUP):
        o_ref[:, h * HEAD_DIM:(h + 1) * HEAD_DIM] = out_t[:, h * tq:(h + 1) * tq].T.astype(BF16)


def _attn_a(proj, *, tq=256, tk=512):
    s_len = proj.shape[0]
    assert s_len % tq == 0 and s_len % tk == 0
    m_cols = A_GROUP * tq
    gw = A_GROUP * HEAD_DIM
    nk = s_len // tk
    kern = functools.partial(_attn_a_kernel, tq=tq, tk=tk, nk=nk)
    return pl.pallas_call(
        kern,
        grid=(A_KV_HEADS, s_len // tq),
        in_specs=[
            pl.BlockSpec((tq, gw), lambda g, i: (i, g)),
            pl.BlockSpec((s_len, HEAD_DIM), lambda g, i: (0, KA_COL + g)),
            pl.BlockSpec((s_len, HEAD_DIM), lambda g, i: (0, VA_COL + g)),
        ],
        out_specs=pl.BlockSpec((tq, gw), lambda g, i: (i, g)),
        out_shape=jax.ShapeDtypeStruct((s_len, A_Q_HEADS * HEAD_DIM), BF16),
        scratch_shapes=[
            pltpu.VMEM((HEAD_DIM, m_cols), BF16),
            pltpu.VMEM((nk, HEAD_DIM + ONES_ROWS, tk), BF16),
            pltpu.SMEM((1,), F32),
            pltpu.VMEM((1, m_cols), F32),
            pltpu.VMEM((HEAD_DIM + ONES_ROWS, m_cols), F32),
        ],
        compiler_params=pltpu.CompilerParams(
            dimension_semantics=("arbitrary", "arbitrary"), vmem_limit_bytes=VMEM_LIMIT),
        name="attn_global",
    )(proj, proj, proj)


def _nb_block_types(rows):
    last_q = rows - NB_QROWS
    return ((0, 0), (2 * NB_QROWS, 2 * NB_QROWS - WIN_ROWS // 2), (last_q, rows - NB_KROWS))


def _attn_b_kernel(rpb_ref, q_ref, k_ref, v_ref, o_ref, tcol_ref, tab_ref, vx_ref, *, rows, ub):
    h = pl.program_id(0)
    step = pl.program_id(1)
    nblk = rows // NB_QROWS
    n_dr = 2 * WIN_ROWS - 1
    n_dc = 2 * WIN_COLS - 1

    @pl.when(step == 0)
    def _build_tables():
        chunk = NB_TQ
        lane_v = lax.broadcasted_iota(jnp.int32, (chunk, HEAD_DIM), 1)
        ones_col = jnp.where(lane_v == 0, 1.0, 0.0).astype(BF16)

        def copy_v(c, carry):
            off_c = pl.multiple_of(c * chunk, chunk)
            vx_ref[pl.ds(off_c, chunk), :HEAD_DIM] = v_ref[pl.ds(off_c, chunk), :]
            vx_ref[pl.ds(off_c, chunk), HEAD_DIM:] = ones_col
            return carry

        lax.fori_loop(0, v_ref.shape[0] // chunk, copy_v, 0)

        c = lax.broadcasted_iota(jnp.int32, (GRID_W, 2 * GRID_W), 0)
        kc = lax.broadcasted_iota(jnp.int32, (GRID_W, 2 * GRID_W), 1) % GRID_W
        c0 = jnp.clip(c - WIN_COLS // 2, 0, GRID_W - WIN_COLS)
        col_ok = (kc >= c0) & (kc < c0 + WIN_COLS)
        dc = kc - c + (WIN_COLS - 1)
        base = h * (n_dr * n_dc)

        def row_body(a, carry):
            t = jnp.full((GRID_W, 2 * GRID_W), NEG_BIG, F32)
            for b in range(n_dc):
                t = jnp.where(dc == b, rpb_ref[base + a * n_dc + b] * LOG2E, t)
            tcol_ref[a] = jnp.where(col_ok, t, NEG_BIG)
            return carry

        lax.fori_loop(0, n_dr, row_body, 0)

        lane = lax.broadcasted_iota(jnp.int32, (GRID_W, 2 * GRID_W), 1)
        neg = jnp.full((GRID_W, 2 * GRID_W), NEG_BIG, F32)
        for t, (r_first, k_first) in enumerate(_nb_block_types(rows)):
            for qr in range(NB_QROWS):
                r = r_first + qr
                r0 = min(max(r - WIN_ROWS // 2, 0), rows - WIN_ROWS)
                for jj in range(NB_KROWS // 2):
                    halves = []
                    for kr in (2 * jj, 2 * jj + 1):
                        k_abs = k_first + kr
                        ok = r0 <= k_abs < r0 + WIN_ROWS
                        halves.append(tcol_ref[k_abs - r + WIN_ROWS - 1] if ok else neg)
                    tab_ref[t, qr * GRID_W:(qr + 1) * GRID_W,
                            jj * 2 * GRID_W:(jj + 1) * 2 * GRID_W] = jnp.where(
                                lane < GRID_W, halves[0], halves[1])

    for u in range(ub):
        pb = step * ub + u
        btype = jnp.where(pb == 0, 0, jnp.where(pb == nblk - 1, 2, 1))
        k_first = jnp.clip(pb * NB_QROWS - WIN_ROWS // 2, 0, rows - NB_KROWS)
        off = pl.multiple_of(k_first * GRID_W, GRID_W)
        ks = k_ref[pl.ds(off, NB_TK), :]
        s = lax.dot_general(q_ref[u * NB_TQ:(u + 1) * NB_TQ, :], ks, (((1,), (1,)), ((), ())),
                            preferred_element_type=F32)
        s = s + tab_ref[btype]
        m = jnp.max(s, axis=1, keepdims=True)
        p = jnp.exp2(s - m).astype(BF16)
        acc = jnp.dot(p, vx_ref[pl.ds(off, NB_TK), :], preferred_element_type=F32)
        o_ref[u * NB_TQ:(u + 1) * NB_TQ, :] = (
            acc[:, :HEAD_DIM] / acc[:, HEAD_DIM:HEAD_DIM + 1]).astype(BF16)


def _attn_b(proj, rpb_flat, *, ub=4):
    s_len = proj.shape[0]
    rows = s_len // GRID_W
    nblk = rows // NB_QROWS
    assert rows % NB_QROWS == 0 and rows >= 3 * NB_QROWS + WIN_ROWS and nblk % ub == 0
    kern = functools.partial(_attn_b_kernel, rows=rows, ub=ub)
    return pl.pallas_call(
        kern,
        grid=(B_HEADS, nblk // ub),
        in_specs=[
            pl.BlockSpec(memory_space=pltpu.SMEM),
            pl.BlockSpec((ub * NB_TQ, HEAD_DIM), lambda h, p: (p, QB_COL + h)),
            pl.BlockSpec((s_len, HEAD_DIM), lambda h, p: (0, KB_COL + h)),
            pl.BlockSpec((s_len, HEAD_DIM), lambda h, p: (0, VB_COL + h)),
        ],
        out_specs=pl.BlockSpec((ub * NB_TQ, HEAD_DIM), lambda h, p: (p, h)),
        out_shape=jax.ShapeDtypeStruct((s_len, B_HEADS * HEAD_DIM), BF16),
        scratch_shapes=[
            pltpu.VMEM((2 * WIN_ROWS - 1, GRID_W, 2 * GRID_W), F32),
            pltpu.VMEM((3, NB_TQ, NB_TK), F32),
            pltpu.VMEM((s_len, 2 * HEAD_DIM), BF16),
        ],
        compiler_params=pltpu.CompilerParams(
            dimension_semantics=("parallel", "arbitrary"), vmem_limit_bytes=VMEM_LIMIT),
        name="attn_nbr",
    )(rpb_flat, proj, proj, proj)


def _oproj_kernel(oa_ref, ob_ref, wo_ref, x_ref, g_ref, h_ref):
    na = oa_ref.shape[1]
    mix = jnp.dot(oa_ref[...], wo_ref[:na, :], preferred_element_type=F32)
    mix = mix + jnp.dot(ob_ref[...], wo_ref[na:, :], preferred_element_type=F32)
    h_ref[...] = x_ref[...] + _rms(mix, g_ref[...])


def _oproj(oa, ob, wo, x, g, *, tm=512):
    s_len, d = x.shape
    assert s_len % tm == 0
    return pl.pallas_call(
        _oproj_kernel,
        grid=(s_len // tm,),
        in_specs=[
            pl.BlockSpec((tm, oa.shape[1]), lambda i: (i, 0)),
            pl.BlockSpec((tm, ob.shape[1]), lambda i: (i, 0)),
            pl.BlockSpec(wo.shape, lambda i: (0, 0)),
            pl.BlockSpec((tm, d), lambda i: (i, 0)),
            pl.BlockSpec((1, d), lambda i: (0, 0)),
        ],
        out_specs=pl.BlockSpec((tm, d), lambda i: (i, 0)),
        out_shape=jax.ShapeDtypeStruct((s_len, d), F32),
        compiler_params=pltpu.CompilerParams(
            dimension_semantics=("parallel",), vmem_limit_bytes=VMEM_LIMIT),
        name="oproj",
    )(oa, ob, wo, x, g)


def _mlp_kernel(h_ref, gpre_ref, wup_ref, wdn_ref, gpost_ref, o_ref, xn_ref, acc_ref):
    f = pl.program_id(1)

    @pl.when(f == 0)
    def _():
        xn_ref[...] = _rms(h_ref[...], gpre_ref[...]).astype(BF16)
        acc_ref[...] = jnp.zeros(acc_ref.shape, F32)

    u = jnp.dot(xn_ref[...], wup_ref[...], preferred_element_type=F32)
    a = jnp.square(jnp.maximum(u, 0.0)).astype(BF16)
    acc_ref[...] += jnp.dot(a, wdn_ref[...], preferred_element_type=F32)

    @pl.when(f == pl.num_programs(1) - 1)
    def _():
        o_ref[...] = h_ref[...] + _rms(acc_ref[...], gpost_ref[...])


def _mlp(h, gpre, wup, wdn, gpost, *, tm=512, tf=1024):
    s_len, d = h.shape
    d_ff = wup.shape[1]
    assert s_len % tm == 0 and d_ff % tf == 0
    return pl.pallas_call(
        _mlp_kernel,
        grid=(s_len // tm, d_ff // tf),
        in_specs=[
            pl.BlockSpec((tm, d), lambda i, f: (i, 0)),
            pl.BlockSpec((1, d), lambda i, f: (0, 0)),
            pl.BlockSpec((d, tf), lambda i, f: (0, f)),
            pl.BlockSpec((tf, d), lambda i, f: (f, 0)),
            pl.BlockSpec((1, d), lambda i, f: (0, 0)),
        ],
        out_specs=pl.BlockSpec((tm, d), lambda i, f: (i, 0)),
        out_shape=jax.ShapeDtypeStruct((s_len, d), F32),
        scratch_shapes=[pltpu.VMEM((tm, d), BF16), pltpu.VMEM((tm, d), F32)],
        compiler_params=pltpu.CompilerParams(
            dimension_semantics=("parallel", "arbitrary"), vmem_limit_bytes=VMEM_LIMIT),
        name="mlp",
    )(h, gpre, wup, wdn, gpost)


def _ple_kernel(h_ref, p_ref, gpre_ref, wg_ref, wp_ref, gpost_ref, o_ref):
    h = h_ref[...]
    xn = _rms(h, gpre_ref[...]).astype(BF16)
    gate = jax.nn.sigmoid(jnp.dot(xn, wg_ref[...], preferred_element_type=F32))
    e = jnp.dot(p_ref[...].astype(BF16), wp_ref[...], preferred_element_type=F32) * gate
    o_ref[...] = h + _rms(e, gpost_ref[...])


def _ple(h, p, gpre, wg, wp, gpost, *, tm=512):
    s_len, d = h.shape
    dp = p.shape[1]
    assert s_len % tm == 0
    return pl.pallas_call(
        _ple_kernel,
        grid=(s_len // tm,),
        in_specs=[
            pl.BlockSpec((tm, d), lambda i: (i, 0)),
            pl.BlockSpec((tm, dp), lambda i: (i, 0)),
            pl.BlockSpec((1, d), lambda i: (0, 0)),
            pl.BlockSpec(wg.shape, lambda i: (0, 0)),
            pl.BlockSpec(wp.shape, lambda i: (0, 0)),
            pl.BlockSpec((1, d), lambda i: (0, 0)),
        ],
        out_specs=pl.BlockSpec((tm, d), lambda i: (i, 0)),
        out_shape=jax.ShapeDtypeStruct((s_len, d), F32),
        compiler_params=pltpu.CompilerParams(
            dimension_semantics=("parallel",), vmem_limit_bytes=VMEM_LIMIT),
        name="ple",
    )(h, p, gpre, wg, wp, gpost)


def _pair_major(a):
    lead = a.shape[:-1]
    return a.reshape(lead + (2, 2, HEAD_DIM // 4)).swapaxes(-3, -2).reshape(lead + (HEAD_DIM,))


def _rope_tables(s_len):
    rows = s_len // GRID_W
    n_freq = HEAD_DIM // 4
    freqs = ROPE_THETA ** (-jnp.arange(n_freq, dtype=F32) / n_freq)
    ar = jnp.arange(rows, dtype=F32)[:, None] * freqs[None, :]
    ac = jnp.arange(GRID_W, dtype=F32)[:, None] * freqs[None, :]
    cr, sr = (jnp.repeat(f(ar), GRID_W, axis=0) for f in (jnp.cos, jnp.sin))
    cc, sc = (jnp.tile(f(ac), (rows, 1)) for f in (jnp.cos, jnp.sin))
    cos = jnp.concatenate([cr, cc, cr, cc], axis=-1)
    sin = jnp.concatenate([-sr, -sc, sr, sc], axis=-1)
    return cos, sin


def kernel(x, p, pre_mix_norm, w_in, q_norm, k_norm, rel_pos_bias, w_o, post_mix_norm,
           pre_mlp_norm, w_up, w_down, post_mlp_norm, pre_ple_norm, w_ple_gate, w_ple_proj,
           post_ple_norm):
    b, s_len, d = x.shape
    depth = w_in.shape[0]
    cos, sin = _rope_tables(s_len)
    outs = []
    for bi in range(b):
        h = x[bi]
        for i in range(depth):
            qn = _pair_major(q_norm[i] * (SM_SCALE * LOG2E)).reshape(1, HEAD_DIM)
            kn = _pair_major(k_norm[i]).reshape(1, HEAD_DIM)
            n_rot = VA_COL * HEAD_DIM
            w_rot = _pair_major(w_in[i][:, :n_rot].reshape(d, VA_COL, HEAD_DIM)).reshape(d, n_rot)
            w_proj = jnp.concatenate([w_rot, w_in[i][:, n_rot:]], axis=1).astype(BF16)
            proj = _inproj(h, pre_mix_norm[i].reshape(1, d), w_proj, cos, sin, qn, kn)
            out_a = _attn_a(proj)
            out_b = _attn_b(proj, rel_pos_bias[i].reshape(-1))
            h = _oproj(out_a, out_b, w_o[i].astype(BF16), h, post_mix_norm[i].reshape(1, d))
            h = _mlp(h, pre_mlp_norm[i].reshape(1, d), w_up[i].astype(BF16),
                     w_down[i].astype(BF16), post_mlp_norm[i].reshape(1, d))
            h = _ple(h, p[i, bi], pre_ple_norm[i].reshape(1, d), w_ple_gate[i].astype(BF16),
                     w_ple_proj[i].astype(BF16), post_ple_norm[i].reshape(1, d))
        outs.append(h)
    return jnp.stack(outs, axis=0)
```

```python
import functools
import math

import jax
import jax.numpy as jnp
from jax import lax
from jax.experimental import pallas as pl
from jax.experimental.pallas import tpu as pltpu

F32 = jnp.float32
BF16 = jnp.bfloat16

HEAD_DIM = 128
GRID_W = 64
A_Q_HEADS = 8
A_KV_HEADS = 2
A_GROUP = A_Q_HEADS // A_KV_HEADS
B_HEADS = 8
WIN_ROWS = 8
WIN_COLS = 16
ROPE_THETA = 10000.0
NORM_EPS = 1e-6
LOG2E = math.log2(math.e)
SM_SCALE = 1.0 / math.sqrt(HEAD_DIM)
NEG_BIG = -1e30

QA_COL, KA_COL, VA_COL = 0, 8, 10
QB_COL, KB_COL, VB_COL = 12, 20, 28
A_COLS = 12 * HEAD_DIM

V7X_VMEM_BYTES = 64 * 1024 * 1024
VMEM_LIMIT = V7X_VMEM_BYTES - 8 * 1024 * 1024

NB_QROWS = 4
NB_KROWS = NB_QROWS + WIN_ROWS
NB_TQ = NB_QROWS * GRID_W
NB_TK = NB_KROWS * GRID_W


def _rms(x, g):
    return x * lax.rsqrt(jnp.mean(x * x, axis=-1, keepdims=True) + NORM_EPS) * g


def _lane_sum_mxu(v):
    hi = v.astype(BF16)
    lo = (v - hi.astype(F32)).astype(BF16)
    ones = jnp.ones((2 * v.shape[1], v.shape[1]), BF16)
    return jnp.dot(jnp.concatenate([hi, lo], axis=1), ones, preferred_element_type=F32)


def _inproj_kernel(x_ref, g_ref, w_ref, crow_ref, srow_ref, ccol_ref, scol_ref, qn_ref, kn_ref,
                   o_ref, xn_ref):
    j = pl.program_id(1)

    @pl.when(j == 0)
    def _():
        xn_ref[...] = _rms(x_ref[...], g_ref[...]).astype(BF16)

    @pl.when(j == 0)
    def _():
        def per_token(row_ref, col_ref):
            lines = [jnp.broadcast_to(row_ref[r:r + 1, :], (GRID_W, HEAD_DIM))
                     for r in range(row_ref.shape[0])]
            return jnp.concatenate(lines, axis=0) + col_ref[...]

        cos = per_token(crow_ref, ccol_ref)
        sin = per_token(srow_ref, scol_ref)
        hpd = 4
        for grp in range(A_COLS // (hpd * HEAD_DIM)):
            c0 = grp * hpd * HEAD_DIM
            acc = jnp.dot(xn_ref[...], w_ref[:, c0:c0 + hpd * HEAD_DIM],
                          preferred_element_type=F32)
            for sub in range(hpd):
                hh = hpd * grp + sub
                blk = acc[:, sub * HEAD_DIM:(sub + 1) * HEAD_DIM]
                if hh < VA_COL:
                    gain = qn_ref[...] if hh < KA_COL else kn_ref[...]
                    y = blk * lax.rsqrt(_lane_sum_mxu(blk * blk) * (1.0 / HEAD_DIM) + NORM_EPS) * gain
                    blk = y * cos + pltpu.roll(y, HEAD_DIM // 2, 1) * sin
                o_ref[:, hh * HEAD_DIM:(hh + 1) * HEAD_DIM] = blk.astype(BF16)

    @pl.when(j == 1)
    def _():
        nq = B_HEADS * HEAD_DIM
        acc = jnp.dot(xn_ref[...], w_ref[...], preferred_element_type=F32)
        o_ref[:, :nq] = (acc[:, :nq] * (SM_SCALE * LOG2E)).astype(BF16)
        o_ref[:, nq:] = acc[:, nq:].astype(BF16)

    @pl.when(j > 1)
    def _():
        o_ref[...] = jnp.dot(xn_ref[...], w_ref[...], preferred_element_type=F32).astype(BF16)


def _inproj(x, g, w, rope, qn, kn, *, tm=1024):
    s_len, d = x.shape
    n = w.shape[1]
    tn = A_COLS
    assert s_len % tm == 0 and n % tn == 0 and tm % GRID_W == 0
    crow, srow, ccol, scol = rope
    tile_rows = tm // GRID_W
    ccol, scol = (jnp.tile(t, (tile_rows, 1)) for t in (ccol, scol))
    return pl.pallas_call(
        _inproj_kernel,
        grid=(s_len // tm, n // tn),
        in_specs=[
            pl.BlockSpec((tm, d), lambda i, j: (i, 0)),
            pl.BlockSpec((1, d), lambda i, j: (0, 0)),
            pl.BlockSpec((d, tn), lambda i, j: (0, j)),
            pl.BlockSpec((tile_rows, HEAD_DIM), lambda i, j: (i, 0)),
            pl.BlockSpec((tile_rows, HEAD_DIM), lambda i, j: (i, 0)),
            pl.BlockSpec((tm, HEAD_DIM), lambda i, j: (0, 0)),
            pl.BlockSpec((tm, HEAD_DIM), lambda i, j: (0, 0)),
            pl.BlockSpec((1, HEAD_DIM), lambda i, j: (0, 0)),
            pl.BlockSpec((1, HEAD_DIM), lambda i, j: (0, 0)),
        ],
        out_specs=pl.BlockSpec((tm, tn), lambda i, j: (i, j)),
        out_shape=jax.ShapeDtypeStruct((s_len, n), BF16),
        scratch_shapes=[pltpu.VMEM((tm, d), BF16)],
        compiler_params=pltpu.CompilerParams(
            dimension_semantics=("parallel", "arbitrary"), vmem_limit_bytes=VMEM_LIMIT),
        name="inproj",
    )(x, g, w, crow, srow, ccol, scol, qn, kn)


EXP2_SAFE_BOUND = 60.0


ONES_ROWS = 16


def _attn_a_kernel(q_ref, k_ref, v_ref, o_ref, qt_ref, vxt_ref, kmax_ref, m_ref, acc_ref,
                   *, tq, tk, nk):
    i = pl.program_id(1)

    @pl.when(i == 0)
    def _prepare_kv():
        row = lax.broadcasted_iota(jnp.int32, (ONES_ROWS, tk), 0)
        ones_rows = jnp.where(row == 0, 1.0, 0.0).astype(BF16)

        def prep(c, kmax):
            off = pl.multiple_of(c * tk, tk)
            vxt_ref[c, :HEAD_DIM, :] = v_ref[pl.ds(off, tk), :].astype(F32).T.astype(BF16)
            vxt_ref[c, HEAD_DIM:, :] = ones_rows
            kf = k_ref[pl.ds(off, tk), :].astype(F32)
            return jnp.maximum(kmax, jnp.sum(kf * kf, axis=1, keepdims=True))

        kmax = lax.fori_loop(0, nk, prep, jnp.zeros((tk, 1), F32))
        kmax_ref[0] = jnp.max(kmax)

    for h in range(A_GROUP):
        qt_ref[:, h * tq:(h + 1) * tq] = (
            q_ref[:, h * HEAD_DIM:(h + 1) * HEAD_DIM].astype(F32).T.astype(BF16))
    acc_ref[...] = jnp.zeros(acc_ref.shape, F32)
    qf = qt_ref[...].astype(F32)
    qmax = jnp.max(jnp.sum(qf * qf, axis=0, keepdims=True))
    no_max_needed = qmax * kmax_ref[0] <= EXP2_SAFE_BOUND * EXP2_SAFE_BOUND

    @pl.when(no_max_needed)
    def _plain():
        def body(j, carry):
            off = pl.multiple_of(j * tk, tk)
            s = jnp.dot(k_ref[pl.ds(off, tk), :], qt_ref[...], preferred_element_type=F32)
            p = jnp.exp2(s).astype(BF16)
            acc_ref[...] += jnp.dot(vxt_ref[j], p, preferred_element_type=F32)
            return carry

        lax.fori_loop(0, nk, body, 0, unroll=16)

    @pl.when(jnp.logical_not(no_max_needed))
    def _online():
        m_ref[...] = jnp.full(m_ref.shape, -jnp.inf, F32)

        def body(j, carry):
            off = pl.multiple_of(j * tk, tk)
            s = jnp.dot(k_ref[pl.ds(off, tk), :], qt_ref[...], preferred_element_type=F32)
            m_prev = m_ref[...]
            m_new = jnp.maximum(m_prev, jnp.max(s, axis=0, keepdims=True))
            p = jnp.exp2(s - m_new).astype(BF16)
            acc_ref[...] = jnp.exp2(m_prev - m_new) * acc_ref[...] + jnp.dot(
                vxt_ref[j], p, preferred_element_type=F32)
            m_ref[...] = m_new
            return carry

        lax.fori_loop(0, nk, body, 0)

    out_t = acc_ref[:HEAD_DIM, :] / acc_ref[HEAD_DIM:HEAD_DIM + 1, :]
    for h in range(A_GROUP):
        o_ref[:, h * HEAD_DIM:(h + 1) * HEAD_DIM] = out_t[:, h * tq:(h + 1) * tq].T.astype(BF16)


def _attn_a(proj, *, tq=256, tk=512):
    s_len = proj.shape[0]
    assert s_len % tq == 0 and s_len % tk == 0
    m_cols = A_GROUP * tq
    gw = A_GROUP * HEAD_DIM
    nk = s_len // tk
    kern = functools.partial(_attn_a_kernel, tq=tq, tk=tk, nk=nk)
    return pl.pallas_call(
        kern,
        grid=(A_KV_HEADS, s_len // tq),
        in_specs=[
            pl.BlockSpec((tq, gw), lambda g, i: (i, g)),
            pl.BlockSpec((s_len, HEAD_DIM), lambda g, i: (0, KA_COL + g)),
            pl.BlockSpec((s_len, HEAD_DIM), lambda g, i: (0, VA_COL + g)),
        ],
        out_specs=pl.BlockSpec((tq, gw), lambda g, i: (i, g)),
        out_shape=jax.ShapeDtypeStruct((s_len, A_Q_HEADS * HEAD_DIM), BF16),
        scratch_shapes=[
            pltpu.VMEM((HEAD_DIM, m_cols), BF16),
            pltpu.VMEM((nk, HEAD_DIM + ONES_ROWS, tk), BF16),
            pltpu.SMEM((1,), F32),
            pltpu.VMEM((1, m_cols), F32),
            pltpu.VMEM((HEAD_DIM + ONES_ROWS, m_cols), F32),
        ],
        compiler_params=pltpu.CompilerParams(
            dimension_semantics=("arbitrary", "arbitrary"), vmem_limit_bytes=VMEM_LIMIT),
        name="attn_global",
    )(proj, proj, proj)


def _nb_block_types(rows):
    last_q = rows - NB_QROWS
    return ((0, 0), (2 * NB_QROWS, 2 * NB_QROWS - WIN_ROWS // 2), (last_q, rows - NB_KROWS))


def _attn_b_kernel(rpb_ref, q_ref, k_ref, v_ref, o_ref, tcol_ref, tab_ref, vx_ref, *, rows, ub):
    h = pl.program_id(0)
    step = pl.program_id(1)
    nblk = rows // NB_QROWS
    n_dr = 2 * WIN_ROWS - 1
    n_dc = 2 * WIN_COLS - 1

    @pl.when(step == 0)
    def _build_tables():
        chunk = NB_TQ
        lane_v = lax.broadcasted_iota(jnp.int32, (chunk, HEAD_DIM), 1)
        ones_col = jnp.where(lane_v == 0, 1.0, 0.0).astype(BF16)

        def copy_v(c, carry):
            off_c = pl.multiple_of(c * chunk, chunk)
            vx_ref[pl.ds(off_c, chunk), :HEAD_DIM] = v_ref[pl.ds(off_c, chunk), :]
            vx_ref[pl.ds(off_c, chunk), HEAD_DIM:] = ones_col
            return carry

        lax.fori_loop(0, v_ref.shape[0] // chunk, copy_v, 0)

        c = lax.broadcasted_iota(jnp.int32, (GRID_W, 2 * GRID_W), 0)
        kc = lax.broadcasted_iota(jnp.int32, (GRID_W, 2 * GRID_W), 1) % GRID_W
        c0 = jnp.clip(c - WIN_COLS // 2, 0, GRID_W - WIN_COLS)
        col_ok = (kc >= c0) & (kc < c0 + WIN_COLS)
        dc = kc - c + (WIN_COLS - 1)
        base = h * (n_dr * n_dc)

        def row_body(a, carry):
            t = jnp.full((GRID_W, 2 * GRID_W), NEG_BIG, F32)
            for b in range(n_dc):
                t = jnp.where(dc == b, rpb_ref[base + a * n_dc + b] * LOG2E, t)
            tcol_ref[a] = jnp.where(col_ok, t, NEG_BIG)
            return carry

        lax.fori_loop(0, n_dr, row_body, 0)

        lane = lax.broadcasted_iota(jnp.int32, (GRID_W, 2 * GRID_W), 1)
        neg = jnp.full((GRID_W, 2 * GRID_W), NEG_BIG, F32)
        for t, (r_first, k_first) in enumerate(_nb_block_types(rows)):
            for qr in range(NB_QROWS):
                r = r_first + qr
                r0 = min(max(r - WIN_ROWS // 2, 0), rows - WIN_ROWS)
                for jj in range(NB_KROWS // 2):
                    halves = []
                    for kr in (2 * jj, 2 * jj + 1):
                        k_abs = k_first + kr
                        ok = r0 <= k_abs < r0 + WIN_ROWS
                        halves.append(tcol_ref[k_abs - r + WIN_ROWS - 1] if ok else neg)
                    tab_ref[t, qr * GRID_W:(qr + 1) * GRID_W,
                            jj * 2 * GRID_W:(jj + 1) * 2 * GRID_W] = jnp.where(
                                lane < GRID_W, halves[0], halves[1])

    for u in range(ub):
        pb = step * ub + u
        btype = jnp.where(pb == 0, 0, jnp.where(pb == nblk - 1, 2, 1))
        k_first = jnp.clip(pb * NB_QROWS - WIN_ROWS // 2, 0, rows - NB_KROWS)
        off = pl.multiple_of(k_first * GRID_W, GRID_W)
        ks = k_ref[pl.ds(off, NB_TK), :]
        s = lax.dot_general(q_ref[u * NB_TQ:(u + 1) * NB_TQ, :], ks, (((1,), (1,)), ((), ())),
                            preferred_element_type=F32)
        s = s + tab_ref[btype]
        m = jnp.max(s, axis=1, keepdims=True)
        p = jnp.exp2(s - m).astype(BF16)
        acc = jnp.dot(p, vx_ref[pl.ds(off, NB_TK), :], preferred_element_type=F32)
        o_ref[u * NB_TQ:(u + 1) * NB_TQ, :] = (
            acc[:, :HEAD_DIM] / acc[:, HEAD_DIM:HEAD_DIM + 1]).astype(BF16)


def _attn_b(proj, rpb_flat, *, ub=8):
    s_len = proj.shape[0]
    rows = s_len // GRID_W
    nblk = rows // NB_QROWS
    assert rows % NB_QROWS == 0 and rows >= 3 * NB_QROWS + WIN_ROWS and nblk % ub == 0
    kern = functools.partial(_attn_b_kernel, rows=rows, ub=ub)
    return pl.pallas_call(
        kern,
        grid=(B_HEADS, nblk // ub),
        in_specs=[
            pl.BlockSpec(memory_space=pltpu.SMEM),
            pl.BlockSpec((ub * NB_TQ, HEAD_DIM), lambda h, p: (p, QB_COL + h)),
            pl.BlockSpec((s_len, HEAD_DIM), lambda h, p: (0, KB_COL + h)),
            pl.BlockSpec((s_len, HEAD_DIM), lambda h, p: (0, VB_COL + h)),
        ],
        out_specs=pl.BlockSpec((ub * NB_TQ, HEAD_DIM), lambda h, p: (p, h)),
        out_shape=jax.ShapeDtypeStruct((s_len, B_HEADS * HEAD_DIM), BF16),
        scratch_shapes=[
            pltpu.VMEM((2 * WIN_ROWS - 1, GRID_W, 2 * GRID_W), F32),
            pltpu.VMEM((3, NB_TQ, NB_TK), F32),
            pltpu.VMEM((s_len, 2 * HEAD_DIM), BF16),
        ],
        compiler_params=pltpu.CompilerParams(
            dimension_semantics=("parallel", "arbitrary"), vmem_limit_bytes=VMEM_LIMIT),
        name="attn_nbr",
    )(rpb_flat, proj, proj, proj)


ROW_SUB = 256


def _row_subblocks(tm):
    return [slice(r, r + ROW_SUB) for r in range(0, tm, ROW_SUB)]


def _oproj_kernel(oa_ref, ob_ref, wo_ref, x_ref, g_ref, h_ref):
    na = oa_ref.shape[1]
    for rows in _row_subblocks(x_ref.shape[0]):
        mix = jnp.dot(oa_ref[rows, :], wo_ref[:na, :], preferred_element_type=F32)
        mix = mix + jnp.dot(ob_ref[rows, :], wo_ref[na:, :], preferred_element_type=F32)
        h_ref[rows, :] = x_ref[rows, :] + _rms(mix, g_ref[...])


def _oproj(oa, ob, wo, x, g, *, tm=1024):
    s_len, d = x.shape
    assert s_len % tm == 0 and tm % ROW_SUB == 0
    return pl.pallas_call(
        _oproj_kernel,
        grid=(s_len // tm,),
        in_specs=[
            pl.BlockSpec((tm, oa.shape[1]), lambda i: (i, 0)),
            pl.BlockSpec((tm, ob.shape[1]), lambda i: (i, 0)),
            pl.BlockSpec(wo.shape, lambda i: (0, 0), pipeline_mode=pl.Buffered(1)),
            pl.BlockSpec((tm, d), lambda i: (i, 0)),
            pl.BlockSpec((1, d), lambda i: (0, 0)),
        ],
        out_specs=pl.BlockSpec((tm, d), lambda i: (i, 0)),
        out_shape=jax.ShapeDtypeStruct((s_len, d), F32),
        compiler_params=pltpu.CompilerParams(
            dimension_semantics=("parallel",), vmem_limit_bytes=VMEM_LIMIT),
        name="oproj",
    )(oa, ob, wo, x, g)


def _mlp_kernel(h_ref, gpre_ref, wup_ref, wdn_ref, gpost_ref, o_ref, xn_ref, acc_ref):
    f = pl.program_id(1)

    def ff_chunk():
        u = jnp.dot(xn_ref[...], wup_ref[...], preferred_element_type=F32)
        a = jnp.square(jnp.maximum(u, 0.0)).astype(BF16)
        return jnp.dot(a, wdn_ref[...], preferred_element_type=F32)

    @pl.when(f == 0)
    def _():
        xn_ref[...] = _rms(h_ref[...], gpre_ref[...]).astype(BF16)
        acc_ref[...] = ff_chunk()

    @pl.when(f > 0)
    def _():
        acc_ref[...] += ff_chunk()

    @pl.when(f == pl.num_programs(1) - 1)
    def _():
        o_ref[...] = h_ref[...] + _rms(acc_ref[...], gpost_ref[...])


def _mlp(h, gpre, wup, wdn, gpost, *, tm=512, tf=1024):
    s_len, d = h.shape
    d_ff = wup.shape[1]
    assert s_len % tm == 0 and d_ff % tf == 0
    return pl.pallas_call(
        _mlp_kernel,
        grid=(s_len // tm, d_ff // tf),
        in_specs=[
            pl.BlockSpec((tm, d), lambda i, f: (i, 0)),
            pl.BlockSpec((1, d), lambda i, f: (0, 0)),
            pl.BlockSpec((d, tf), lambda i, f: (0, f)),
            pl.BlockSpec((tf, d), lambda i, f: (f, 0)),
            pl.BlockSpec((1, d), lambda i, f: (0, 0)),
        ],
        out_specs=pl.BlockSpec((tm, d), lambda i, f: (i, 0)),
        out_shape=jax.ShapeDtypeStruct((s_len, d), F32),
        scratch_shapes=[pltpu.VMEM((tm, d), BF16), pltpu.VMEM((tm, d), F32)],
        compiler_params=pltpu.CompilerParams(
            dimension_semantics=("parallel", "arbitrary"), vmem_limit_bytes=VMEM_LIMIT),
        name="mlp",
    )(h, gpre, wup, wdn, gpost)


def _ple_kernel(h_ref, p_ref, gpre_ref, wg_ref, wp_ref, gpost_ref, o_ref):
    for rows in _row_subblocks(h_ref.shape[0]):
        h = h_ref[rows, :]
        xn = _rms(h, gpre_ref[...]).astype(BF16)
        gate = jax.nn.sigmoid(jnp.dot(xn, wg_ref[...], preferred_element_type=F32))
        e = jnp.dot(p_ref[rows, :].astype(BF16), wp_ref[...], preferred_element_type=F32) * gate
        o_ref[rows, :] = h + _rms(e, gpost_ref[...])


def _ple(h, p, gpre, wg, wp, gpost, *, tm=1024):
    s_len, d = h.shape
    dp = p.shape[1]
    assert s_len % tm == 0 and tm % ROW_SUB == 0
    return pl.pallas_call(
        _ple_kernel,
        grid=(s_len // tm,),
        in_specs=[
            pl.BlockSpec((tm, d), lambda i: (i, 0)),
            pl.BlockSpec((tm, dp), lambda i: (i, 0)),
            pl.BlockSpec((1, d), lambda i: (0, 0)),
            pl.BlockSpec(wg.shape, lambda i: (0, 0), pipeline_mode=pl.Buffered(1)),
            pl.BlockSpec(wp.shape, lambda i: (0, 0), pipeline_mode=pl.Buffered(1)),
            pl.BlockSpec((1, d), lambda i: (0, 0)),
        ],
        out_specs=pl.BlockSpec((tm, d), lambda i: (i, 0)),
        out_shape=jax.ShapeDtypeStruct((s_len, d), F32),
        compiler_params=pltpu.CompilerParams(
            dimension_semantics=("parallel",), vmem_limit_bytes=VMEM_LIMIT),
        name="ple",
    )(h, p, gpre, wg, wp, gpost)


def _pair_major(a):
    lead = a.shape[:-1]
    return a.reshape(lead + (2, 2, HEAD_DIM // 4)).swapaxes(-3, -2).reshape(lead + (HEAD_DIM,))


def _rope_tables(s_len):
    rows = s_len // GRID_W
    n_freq = HEAD_DIM // 4
    freqs = ROPE_THETA ** (-jnp.arange(n_freq, dtype=F32) / n_freq)
    ar = jnp.arange(rows, dtype=F32)[:, None] * freqs[None, :]
    ac = jnp.arange(GRID_W, dtype=F32)[:, None] * freqs[None, :]
    zr, zc = jnp.zeros_like(ar), jnp.zeros_like(ac)
    crow = jnp.concatenate([jnp.cos(ar), zr, jnp.cos(ar), zr], axis=-1)
    srow = jnp.concatenate([-jnp.sin(ar), zr, jnp.sin(ar), zr], axis=-1)
    ccol = jnp.concatenate([zc, jnp.cos(ac), zc, jnp.cos(ac)], axis=-1)
    scol = jnp.concatenate([zc, -jnp.sin(ac), zc, jnp.sin(ac)], axis=-1)
    return crow, srow, ccol, scol


def kernel(x, p, pre_mix_norm, w_in, q_norm, k_norm, rel_pos_bias, w_o, post_mix_norm,
           pre_mlp_norm, w_up, w_down, post_mlp_norm, pre_ple_norm, w_ple_gate, w_ple_proj,
           post_ple_norm):
    b, s_len, d = x.shape
    depth = w_in.shape[0]
    rope = _rope_tables(s_len)
    outs = []
    for bi in range(b):
        h = x[bi]
        for i in range(depth):
            qn = _pair_major(q_norm[i] * (SM_SCALE * LOG2E)).reshape(1, HEAD_DIM)
            kn = _pair_major(k_norm[i]).reshape(1, HEAD_DIM)
            n_rot = VA_COL * HEAD_DIM
            w_rot = _pair_major(w_in[i][:, :n_rot].reshape(d, VA_COL, HEAD_DIM)).reshape(d, n_rot)
            w_proj = jnp.concatenate([w_rot, w_in[i][:, n_rot:]], axis=1).astype(BF16)
            proj = _inproj(h, pre_mix_norm[i].reshape(1, d), w_proj, rope, qn, kn)
            out_a = _attn_a(proj)
            out_b = _attn_b(proj, rel_pos_bias[i].reshape(-1))
            h = _oproj(out_a, out_b, w_o[i].astype(BF16), h, post_mix_norm[i].reshape(1, d))
            h = _mlp(h, pre_mlp_norm[i].reshape(1, d), w_up[i].astype(BF16),
                     w_down[i].astype(BF16), post_mlp_norm[i].reshape(1, d))
            h = _ple(h, p[i, bi], pre_ple_norm[i].reshape(1, d), w_ple_gate[i].astype(BF16),
                     w_ple_proj[i].astype(BF16), post_ple_norm[i].reshape(1, d))
        outs.append(h)
    return jnp.stack(outs, axis=0)
```

```python
import functools
import math

import jax
import jax.numpy as jnp
from jax import lax
from jax.experimental import pallas as pl
from jax.experimental.pallas import tpu as pltpu

F32 = jnp.float32
BF16 = jnp.bfloat16

HEAD_DIM = 128
GRID_W = 64
A_Q_HEADS = 8
A_KV_HEADS = 2
A_GROUP = A_Q_HEADS // A_KV_HEADS
B_HEADS = 8
WIN_ROWS = 8
WIN_COLS = 16
ROPE_THETA = 10000.0
NORM_EPS = 1e-6
LOG2E = math.log2(math.e)
SM_SCALE = 1.0 / math.sqrt(HEAD_DIM)
NEG_BIG = -1e30

QA_COL, KA_COL, VA_COL = 0, 8, 10
QB_COL, KB_COL, VB_COL = 12, 20, 28
A_COLS = 12 * HEAD_DIM

V7X_VMEM_BYTES = 64 * 1024 * 1024
VMEM_LIMIT = V7X_VMEM_BYTES - 8 * 1024 * 1024

NB_QROWS = 4
NB_KROWS = NB_QROWS + WIN_ROWS
NB_TQ = NB_QROWS * GRID_W
NB_TK = NB_KROWS * GRID_W


def _rms(x, g):
    return x * lax.rsqrt(jnp.mean(x * x, axis=-1, keepdims=True) + NORM_EPS) * g


def _lane_sum_mxu(v):
    hi = v.astype(BF16)
    lo = (v - hi.astype(F32)).astype(BF16)
    ones = jnp.ones((2 * v.shape[1], v.shape[1]), BF16)
    return jnp.dot(jnp.concatenate([hi, lo], axis=1), ones, preferred_element_type=F32)


def _inproj_kernel(x_ref, g_ref, w_ref, crow_ref, srow_ref, ccol_ref, scol_ref, qn_ref, kn_ref,
                   o_ref, xn_ref):
    j = pl.program_id(1)

    @pl.when(j == 0)
    def _():
        xn_ref[...] = _rms(x_ref[...], g_ref[...]).astype(BF16)

    @pl.when(j == 0)
    def _():
        def per_token(row_ref, col_ref):
            lines = [jnp.broadcast_to(row_ref[r:r + 1, :], (GRID_W, HEAD_DIM))
                     for r in range(row_ref.shape[0])]
            return jnp.concatenate(lines, axis=0) + col_ref[...]

        cos = per_token(crow_ref, ccol_ref)
        sin = per_token(srow_ref, scol_ref)
        hpd = 4
        for grp in range(A_COLS // (hpd * HEAD_DIM)):
            c0 = grp * hpd * HEAD_DIM
            acc = jnp.dot(xn_ref[...], w_ref[:, c0:c0 + hpd * HEAD_DIM],
                          preferred_element_type=F32)
            for sub in range(hpd):
                hh = hpd * grp + sub
                blk = acc[:, sub * HEAD_DIM:(sub + 1) * HEAD_DIM]
                if hh < VA_COL:
                    gain = qn_ref[...] if hh < KA_COL else kn_ref[...]
                    y = blk * lax.rsqrt(_lane_sum_mxu(blk * blk) * (1.0 / HEAD_DIM) + NORM_EPS) * gain
                    blk = y * cos + pltpu.roll(y, HEAD_DIM // 2, 1) * sin
                o_ref[:, hh * HEAD_DIM:(hh + 1) * HEAD_DIM] = blk.astype(BF16)

    @pl.when(j == 1)
    def _():
        nq = B_HEADS * HEAD_DIM
        acc = jnp.dot(xn_ref[...], w_ref[...], preferred_element_type=F32)
        o_ref[:, :nq] = (acc[:, :nq] * (SM_SCALE * LOG2E)).astype(BF16)
        o_ref[:, nq:] = acc[:, nq:].astype(BF16)

    @pl.when(j > 1)
    def _():
        o_ref[...] = jnp.dot(xn_ref[...], w_ref[...], preferred_element_type=F32).astype(BF16)


def _inproj(x, g, w, rope, qn, kn, *, tm=1024):
    s_len, d = x.shape
    n = w.shape[1]
    tn = A_COLS
    assert s_len % tm == 0 and n % tn == 0 and tm % GRID_W == 0
    crow, srow, ccol, scol = rope
    tile_rows = tm // GRID_W
    ccol, scol = (jnp.tile(t, (tile_rows, 1)) for t in (ccol, scol))
    return pl.pallas_call(
        _inproj_kernel,
        grid=(s_len // tm, n // tn),
        in_specs=[
            pl.BlockSpec((tm, d), lambda i, j: (i, 0)),
            pl.BlockSpec((1, d), lambda i, j: (0, 0)),
            pl.BlockSpec((d, tn), lambda i, j: (0, j)),
            pl.BlockSpec((tile_rows, HEAD_DIM), lambda i, j: (i, 0)),
            pl.BlockSpec((tile_rows, HEAD_DIM), lambda i, j: (i, 0)),
            pl.BlockSpec((tm, HEAD_DIM), lambda i, j: (0, 0)),
            pl.BlockSpec((tm, HEAD_DIM), lambda i, j: (0, 0)),
            pl.BlockSpec((1, HEAD_DIM), lambda i, j: (0, 0)),
            pl.BlockSpec((1, HEAD_DIM), lambda i, j: (0, 0)),
        ],
        out_specs=pl.BlockSpec((tm, tn), lambda i, j: (i, j)),
        out_shape=jax.ShapeDtypeStruct((s_len, n), BF16),
        scratch_shapes=[pltpu.VMEM((tm, d), BF16)],
        compiler_params=pltpu.CompilerParams(
            dimension_semantics=("parallel", "arbitrary"), vmem_limit_bytes=VMEM_LIMIT),
        name="inproj",
    )(x, g, w, crow, srow, ccol, scol, qn, kn)


EXP2_SAFE_BOUND = 60.0


ONES_ROWS = 16


def _attn_a_kernel(*refs, tq, tk, nk, n_cast):
    q_ref, k_ref, v_ref = refs[:3]
    w32_refs = refs[3:3 + n_cast]
    o_ref = refs[3 + n_cast]
    w16_refs = refs[4 + n_cast:4 + 2 * n_cast]
    qt_ref, vxt_ref, kmax_ref, m_ref, acc_ref = refs[4 + 2 * n_cast:]
    i = pl.program_id(1)
    for w32, w16 in zip(w32_refs, w16_refs):
        w16[...] = w32[...].astype(BF16)

    @pl.when(i == 0)
    def _prepare_kv():
        row = lax.broadcasted_iota(jnp.int32, (ONES_ROWS, tk), 0)
        ones_rows = jnp.where(row == 0, 1.0, 0.0).astype(BF16)

        def prep(c, kmax):
            off = pl.multiple_of(c * tk, tk)
            vxt_ref[c, :HEAD_DIM, :] = v_ref[pl.ds(off, tk), :].astype(F32).T.astype(BF16)
            vxt_ref[c, HEAD_DIM:, :] = ones_rows
            kf = k_ref[pl.ds(off, tk), :].astype(F32)
            return jnp.maximum(kmax, jnp.sum(kf * kf, axis=1, keepdims=True))

        kmax = lax.fori_loop(0, nk, prep, jnp.zeros((tk, 1), F32))
        kmax_ref[0] = jnp.max(kmax)

    for h in range(A_GROUP):
        qt_ref[:, h * tq:(h + 1) * tq] = (
            q_ref[:, h * HEAD_DIM:(h + 1) * HEAD_DIM].astype(F32).T.astype(BF16))
    acc_ref[...] = jnp.zeros(acc_ref.shape, F32)
    qf = qt_ref[...].astype(F32)
    qmax = jnp.max(jnp.sum(qf * qf, axis=0, keepdims=True))
    no_max_needed = qmax * kmax_ref[0] <= EXP2_SAFE_BOUND * EXP2_SAFE_BOUND

    @pl.when(no_max_needed)
    def _plain():
        def body(j, carry):
            off = pl.multiple_of(j * tk, tk)
            s = jnp.dot(k_ref[pl.ds(off, tk), :], qt_ref[...], preferred_element_type=F32)
            p = jnp.exp2(s).astype(BF16)
            acc_ref[...] += jnp.dot(vxt_ref[j], p, preferred_element_type=F32)
            return carry

        lax.fori_loop(0, nk, body, 0, unroll=16)

    @pl.when(jnp.logical_not(no_max_needed))
    def _online():
        m_ref[...] = jnp.full(m_ref.shape, -jnp.inf, F32)

        def body(j, carry):
            off = pl.multiple_of(j * tk, tk)
            s = jnp.dot(k_ref[pl.ds(off, tk), :], qt_ref[...], preferred_element_type=F32)
            m_prev = m_ref[...]
            m_new = jnp.maximum(m_prev, jnp.max(s, axis=0, keepdims=True))
            p = jnp.exp2(s - m_new).astype(BF16)
            acc_ref[...] = jnp.exp2(m_prev - m_new) * acc_ref[...] + jnp.dot(
                vxt_ref[j], p, preferred_element_type=F32)
            m_ref[...] = m_new
            return carry

        lax.fori_loop(0, nk, body, 0)

    out_t = acc_ref[:HEAD_DIM, :] / acc_ref[HEAD_DIM:HEAD_DIM + 1, :]
    for h in range(A_GROUP):
        o_ref[:, h * HEAD_DIM:(h + 1) * HEAD_DIM] = out_t[:, h * tq:(h + 1) * tq].T.astype(BF16)


def _attn_a(proj, cast_ws, *, tq=512, tk=512):
    s_len = proj.shape[0]
    assert s_len % tq == 0 and s_len % tk == 0
    m_cols = A_GROUP * tq
    gw = A_GROUP * HEAD_DIM
    nk = s_len // tk
    n_i = s_len // tq
    n_steps = A_KV_HEADS * n_i
    bf16_rows = 16
    slabs = []
    for w in cast_ws:
        assert w.shape[0] % (n_steps * bf16_rows) == 0
        slabs.append(pl.BlockSpec((w.shape[0] // n_steps, w.shape[1]),
                                  lambda g, i: (g * n_i + i, 0)))
    kern = functools.partial(_attn_a_kernel, tq=tq, tk=tk, nk=nk, n_cast=len(cast_ws))
    outs = pl.pallas_call(
        kern,
        grid=(A_KV_HEADS, n_i),
        in_specs=[
            pl.BlockSpec((tq, gw), lambda g, i: (i, g)),
            pl.BlockSpec((s_len, HEAD_DIM), lambda g, i: (0, KA_COL + g)),
            pl.BlockSpec((s_len, HEAD_DIM), lambda g, i: (0, VA_COL + g)),
        ] + slabs,
        out_specs=[pl.BlockSpec((tq, gw), lambda g, i: (i, g))] + slabs,
        out_shape=[jax.ShapeDtypeStruct((s_len, A_Q_HEADS * HEAD_DIM), BF16)]
        + [jax.ShapeDtypeStruct(w.shape, BF16) for w in cast_ws],
        scratch_shapes=[
            pltpu.VMEM((HEAD_DIM, m_cols), BF16),
            pltpu.VMEM((nk, HEAD_DIM + ONES_ROWS, tk), BF16),
            pltpu.SMEM((1,), F32),
            pltpu.VMEM((1, m_cols), F32),
            pltpu.VMEM((HEAD_DIM + ONES_ROWS, m_cols), F32),
        ],
        compiler_params=pltpu.CompilerParams(
            dimension_semantics=("arbitrary", "arbitrary"), vmem_limit_bytes=VMEM_LIMIT),
        name="attn_global",
    )(proj, proj, proj, *cast_ws)
    return outs[0], tuple(outs[1:])


def _nb_block_types(rows):
    last_q = rows - NB_QROWS
    return ((0, 0), (2 * NB_QROWS, 2 * NB_QROWS - WIN_ROWS // 2), (last_q, rows - NB_KROWS))


def _attn_b_kernel(rpb_ref, q_ref, k_ref, v_ref, o_ref, tcol_ref, tab_ref, vx_ref, *, rows, ub):
    h = pl.program_id(0)
    step = pl.program_id(1)
    nblk = rows // NB_QROWS
    n_dr = 2 * WIN_ROWS - 1
    n_dc = 2 * WIN_COLS - 1

    @pl.when(step == 0)
    def _build_tables():
        chunk = NB_TQ
        lane_v = lax.broadcasted_iota(jnp.int32, (chunk, HEAD_DIM), 1)
        ones_col = jnp.where(lane_v == 0, 1.0, 0.0).astype(BF16)

        def copy_v(c, carry):
            off_c = pl.multiple_of(c * chunk, chunk)
            vx_ref[pl.ds(off_c, chunk), :HEAD_DIM] = v_ref[pl.ds(off_c, chunk), :]
            vx_ref[pl.ds(off_c, chunk), HEAD_DIM:] = ones_col
            return carry

        lax.fori_loop(0, v_ref.shape[0] // chunk, copy_v, 0)

        c = lax.broadcasted_iota(jnp.int32, (GRID_W, 2 * GRID_W), 0)
        kc = lax.broadcasted_iota(jnp.int32, (GRID_W, 2 * GRID_W), 1) % GRID_W
        c0 = jnp.clip(c - WIN_COLS // 2, 0, GRID_W - WIN_COLS)
        col_ok = (kc >= c0) & (kc < c0 + WIN_COLS)
        dc = kc - c + (WIN_COLS - 1)
        base = h * (n_dr * n_dc)

        def row_body(a, carry):
            t = jnp.full((GRID_W, 2 * GRID_W), NEG_BIG, F32)
            for b in range(n_dc):
                t = jnp.where(dc == b, rpb_ref[base + a * n_dc + b] * LOG2E, t)
            tcol_ref[a] = jnp.where(col_ok, t, NEG_BIG)
            return carry

        lax.fori_loop(0, n_dr, row_body, 0)

        lane = lax.broadcasted_iota(jnp.int32, (GRID_W, 2 * GRID_W), 1)
        neg = jnp.full((GRID_W, 2 * GRID_W), NEG_BIG, F32)
        for t, (r_first, k_first) in enumerate(_nb_block_types(rows)):
            for qr in range(NB_QROWS):
                r = r_first + qr
                r0 = min(max(r - WIN_ROWS // 2, 0), rows - WIN_ROWS)
                for jj in range(NB_KROWS // 2):
                    halves = []
                    for kr in (2 * jj, 2 * jj + 1):
                        k_abs = k_first + kr
                        ok = r0 <= k_abs < r0 + WIN_ROWS
                        halves.append(tcol_ref[k_abs - r + WIN_ROWS - 1] if ok else neg)
                    tab_ref[t, qr * GRID_W:(qr + 1) * GRID_W,
                            jj * 2 * GRID_W:(jj + 1) * 2 * GRID_W] = jnp.where(
                                lane < GRID_W, halves[0], halves[1])

    for u in range(ub):
        pb = step * ub + u
        btype = jnp.where(pb == 0, 0, jnp.where(pb == nblk - 1, 2, 1))
        k_first = jnp.clip(pb * NB_QROWS - WIN_ROWS // 2, 0, rows - NB_KROWS)
        off = pl.multiple_of(k_first * GRID_W, GRID_W)
        ks = k_ref[pl.ds(off, NB_TK), :]
        s = lax.dot_general(q_ref[u * NB_TQ:(u + 1) * NB_TQ, :], ks, (((1,), (1,)), ((), ())),
                            preferred_element_type=F32)
        s = s + tab_ref[btype]
        m = jnp.max(s, axis=1, keepdims=True)
        p = jnp.exp2(s - m).astype(BF16)
        acc = jnp.dot(p, vx_ref[pl.ds(off, NB_TK), :], preferred_element_type=F32)
        o_ref[u * NB_TQ:(u + 1) * NB_TQ, :] = (
            acc[:, :HEAD_DIM] / acc[:, HEAD_DIM:HEAD_DIM + 1]).astype(BF16)


def _attn_b(proj, rpb_flat, *, ub=8):
    s_len = proj.shape[0]
    rows = s_len // GRID_W
    nblk = rows // NB_QROWS
    assert rows % NB_QROWS == 0 and rows >= 3 * NB_QROWS + WIN_ROWS and nblk % ub == 0
    kern = functools.partial(_attn_b_kernel, rows=rows, ub=ub)
    return pl.pallas_call(
        kern,
        grid=(B_HEADS, nblk // ub),
        in_specs=[
            pl.BlockSpec(memory_space=pltpu.SMEM),
            pl.BlockSpec((ub * NB_TQ, HEAD_DIM), lambda h, p: (p, QB_COL + h)),
            pl.BlockSpec((s_len, HEAD_DIM), lambda h, p: (0, KB_COL + h)),
            pl.BlockSpec((s_len, HEAD_DIM), lambda h, p: (0, VB_COL + h)),
        ],
        out_specs=pl.BlockSpec((ub * NB_TQ, HEAD_DIM), lambda h, p: (p, h)),
        out_shape=jax.ShapeDtypeStruct((s_len, B_HEADS * HEAD_DIM), BF16),
        scratch_shapes=[
            pltpu.VMEM((2 * WIN_ROWS - 1, GRID_W, 2 * GRID_W), F32),
            pltpu.VMEM((3, NB_TQ, NB_TK), F32),
            pltpu.VMEM((s_len, 2 * HEAD_DIM), BF16),
        ],
        compiler_params=pltpu.CompilerParams(
            dimension_semantics=("parallel", "arbitrary"), vmem_limit_bytes=VMEM_LIMIT),
        name="attn_nbr",
    )(rpb_flat, proj, proj, proj)


ROW_SUB = 256


def _row_subblocks(tm):
    return [slice(r, r + ROW_SUB) for r in range(0, tm, ROW_SUB)]


def _oproj_kernel(oa_ref, ob_ref, wo_ref, x_ref, g_ref, h_ref):
    na = oa_ref.shape[1]
    for rows in _row_subblocks(x_ref.shape[0]):
        mix = jnp.dot(oa_ref[rows, :], wo_ref[:na, :], preferred_element_type=F32)
        mix = mix + jnp.dot(ob_ref[rows, :], wo_ref[na:, :], preferred_element_type=F32)
        h_ref[rows, :] = x_ref[rows, :] + _rms(mix, g_ref[...])


def _oproj(oa, ob, wo, x, g, *, tm=512):
    s_len, d = x.shape
    assert s_len % tm == 0 and tm % ROW_SUB == 0
    return pl.pallas_call(
        _oproj_kernel,
        grid=(s_len // tm,),
        in_specs=[
            pl.BlockSpec((tm, oa.shape[1]), lambda i: (i, 0)),
            pl.BlockSpec((tm, ob.shape[1]), lambda i: (i, 0)),
            pl.BlockSpec(wo.shape, lambda i: (0, 0), pipeline_mode=pl.Buffered(1)),
            pl.BlockSpec((tm, d), lambda i: (i, 0)),
            pl.BlockSpec((1, d), lambda i: (0, 0)),
        ],
        out_specs=pl.BlockSpec((tm, d), lambda i: (i, 0)),
        out_shape=jax.ShapeDtypeStruct((s_len, d), F32),
        compiler_params=pltpu.CompilerParams(
            dimension_semantics=("parallel",), vmem_limit_bytes=VMEM_LIMIT),
        name="oproj",
    )(oa, ob, wo, x, g)


def _mlp_kernel(h_ref, gpre_ref, wup_ref, wdn_ref, gpost_ref, o_ref, xn_ref, acc_ref):
    f = pl.program_id(1)

    def ff_chunk():
        u = jnp.dot(xn_ref[...], wup_ref[...], preferred_element_type=F32)
        a = jnp.square(jnp.maximum(u, 0.0)).astype(BF16)
        return jnp.dot(a, wdn_ref[...], preferred_element_type=F32)

    @pl.when(f == 0)
    def _():
        xn_ref[...] = _rms(h_ref[...], gpre_ref[...]).astype(BF16)
        acc_ref[...] = ff_chunk()

    @pl.when(f > 0)
    def _():
        acc_ref[...] += ff_chunk()

    @pl.when(f == pl.num_programs(1) - 1)
    def _():
        o_ref[...] = h_ref[...] + _rms(acc_ref[...], gpost_ref[...])


def _mlp(h, gpre, wup, wdn, gpost, *, tm=512, tf=1024):
    s_len, d = h.shape
    d_ff = wup.shape[1]
    assert s_len % tm == 0 and d_ff % tf == 0
    return pl.pallas_call(
        _mlp_kernel,
        grid=(s_len // tm, d_ff // tf),
        in_specs=[
            pl.BlockSpec((tm, d), lambda i, f: (i, 0)),
            pl.BlockSpec((1, d), lambda i, f: (0, 0)),
            pl.BlockSpec((d, tf), lambda i, f: (0, f)),
            pl.BlockSpec((tf, d), lambda i, f: (f, 0)),
            pl.BlockSpec((1, d), lambda i, f: (0, 0)),
        ],
        out_specs=pl.BlockSpec((tm, d), lambda i, f: (i, 0)),
        out_shape=jax.ShapeDtypeStruct((s_len, d), F32),
        scratch_shapes=[pltpu.VMEM((tm, d), BF16), pltpu.VMEM((tm, d), F32)],
        compiler_params=pltpu.CompilerParams(
            dimension_semantics=("parallel", "arbitrary"), vmem_limit_bytes=VMEM_LIMIT),
        name="mlp",
    )(h, gpre, wup, wdn, gpost)


def _ple_kernel(h_ref, p_ref, gpre_ref, wg_ref, wp_ref, gpost_ref, o_ref):
    for rows in _row_subblocks(h_ref.shape[0]):
        h = h_ref[rows, :]
        xn = _rms(h, gpre_ref[...]).astype(BF16)
        gate = jax.nn.sigmoid(jnp.dot(xn, wg_ref[...], preferred_element_type=F32))
        e = jnp.dot(p_ref[rows, :].astype(BF16), wp_ref[...], preferred_element_type=F32) * gate
        o_ref[rows, :] = h + _rms(e, gpost_ref[...])


def _ple(h, p, gpre, wg, wp, gpost, *, tm=512):
    s_len, d = h.shape
    dp = p.shape[1]
    assert s_len % tm == 0 and tm % ROW_SUB == 0
    return pl.pallas_call(
        _ple_kernel,
        grid=(s_len // tm,),
        in_specs=[
            pl.BlockSpec((tm, d), lambda i: (i, 0)),
            pl.BlockSpec((tm, dp), lambda i: (i, 0)),
            pl.BlockSpec((1, d), lambda i: (0, 0)),
            pl.BlockSpec(wg.shape, lambda i: (0, 0), pipeline_mode=pl.Buffered(1)),
            pl.BlockSpec(wp.shape, lambda i: (0, 0), pipeline_mode=pl.Buffered(1)),
            pl.BlockSpec((1, d), lambda i: (0, 0)),
        ],
        out_specs=pl.BlockSpec((tm, d), lambda i: (i, 0)),
        out_shape=jax.ShapeDtypeStruct((s_len, d), F32),
        compiler_params=pltpu.CompilerParams(
            dimension_semantics=("parallel",), vmem_limit_bytes=VMEM_LIMIT),
        name="ple",
    )(h, p, gpre, wg, wp, gpost)


def _pair_major(a):
    lead = a.shape[:-1]
    return a.reshape(lead + (2, 2, HEAD_DIM // 4)).swapaxes(-3, -2).reshape(lead + (HEAD_DIM,))


def _rope_tables(s_len):
    rows = s_len // GRID_W
    n_freq = HEAD_DIM // 4
    freqs = ROPE_THETA ** (-jnp.arange(n_freq, dtype=F32) / n_freq)
    ar = jnp.arange(rows, dtype=F32)[:, None] * freqs[None, :]
    ac = jnp.arange(GRID_W, dtype=F32)[:, None] * freqs[None, :]
    zr, zc = jnp.zeros_like(ar), jnp.zeros_like(ac)
    crow = jnp.concatenate([jnp.cos(ar), zr, jnp.cos(ar), zr], axis=-1)
    srow = jnp.concatenate([-jnp.sin(ar), zr, jnp.sin(ar), zr], axis=-1)
    ccol = jnp.concatenate([zc, jnp.cos(ac), zc, jnp.cos(ac)], axis=-1)
    scol = jnp.concatenate([zc, -jnp.sin(ac), zc, jnp.sin(ac)], axis=-1)
    return crow, srow, ccol, scol


def kernel(x, p, pre_mix_norm, w_in, q_norm, k_norm, rel_pos_bias, w_o, post_mix_norm,
           pre_mlp_norm, w_up, w_down, post_mlp_norm, pre_ple_norm, w_ple_gate, w_ple_proj,
           post_ple_norm):
    b, s_len, d = x.shape
    depth = w_in.shape[0]
    rope = _rope_tables(s_len)
    outs = []
    for bi in range(b):
        h = x[bi]
        for i in range(depth):
            qn = _pair_major(q_norm[i] * (SM_SCALE * LOG2E)).reshape(1, HEAD_DIM)
            kn = _pair_major(k_norm[i]).reshape(1, HEAD_DIM)
            n_rot = VA_COL * HEAD_DIM
            w_rot = _pair_major(w_in[i][:, :n_rot].reshape(d, VA_COL, HEAD_DIM)).reshape(d, n_rot)
            w_proj = jnp.concatenate([w_rot, w_in[i][:, n_rot:]], axis=1).astype(BF16)
            proj = _inproj(h, pre_mix_norm[i].reshape(1, d), w_proj, rope, qn, kn)
            out_a, (wo16, wup16, wdn16, wg16) = _attn_a(
                proj, (w_o[i], w_up[i], w_down[i], w_ple_gate[i]))
            out_b = _attn_b(proj, rel_pos_bias[i].reshape(-1))
            h = _oproj(out_a, out_b, wo16, h, post_mix_norm[i].reshape(1, d))
            h = _mlp(h, pre_mlp_norm[i].reshape(1, d), wup16, wdn16, post_mlp_norm[i].reshape(1, d))
            h = _ple(h, p[i, bi], pre_ple_norm[i].reshape(1, d), wg16,
                     w_ple_proj[i].astype(BF16), post_ple_norm[i].reshape(1, d))
        outs.append(h)
    return jnp.stack(outs, axis=0)
```

```python
import functools
import math

import jax
import jax.numpy as jnp
from jax import lax
from jax.experimental import pallas as pl
from jax.experimental.pallas import tpu as pltpu

F32 = jnp.float32
BF16 = jnp.bfloat16

HEAD_DIM = 128
GRID_W = 64
A_Q_HEADS = 8
A_KV_HEADS = 2
A_GROUP = A_Q_HEADS // A_KV_HEADS
B_HEADS = 8
WIN_ROWS = 8
WIN_COLS = 16
ROPE_THETA = 10000.0
NORM_EPS = 1e-6
LOG2E = math.log2(math.e)
SM_SCALE = 1.0 / math.sqrt(HEAD_DIM)
NEG_BIG = -1e30

QA_COL, KA_COL, VA_COL = 0, 8, 10
QB_COL, KB_COL, VB_COL = 12, 20, 28
A_COLS = 12 * HEAD_DIM

V7X_VMEM_BYTES = 64 * 1024 * 1024
VMEM_LIMIT = V7X_VMEM_BYTES - 8 * 1024 * 1024

NB_QROWS = 4
NB_KROWS = NB_QROWS + WIN_ROWS
NB_TQ = NB_QROWS * GRID_W
NB_TK = NB_KROWS * GRID_W


def _rms(x, g):
    return x * lax.rsqrt(jnp.mean(x * x, axis=-1, keepdims=True) + NORM_EPS) * g


def _lane_sum_mxu(v):
    hi = v.astype(BF16)
    lo = (v - hi.astype(F32)).astype(BF16)
    ones = jnp.ones((2 * v.shape[1], v.shape[1]), BF16)
    return jnp.dot(jnp.concatenate([hi, lo], axis=1), ones, preferred_element_type=F32)


def _inproj_kernel(x_ref, g_ref, w_ref, crow_ref, srow_ref, ccol_ref, scol_ref, qn_ref, kn_ref,
                   o_ref, xn_ref):
    j = pl.program_id(1)

    @pl.when(j == 0)
    def _():
        xn_ref[...] = _rms(x_ref[...], g_ref[...]).astype(BF16)

    @pl.when(j == 0)
    def _():
        def per_token(row_ref, col_ref):
            lines = [jnp.broadcast_to(row_ref[r:r + 1, :], (GRID_W, HEAD_DIM))
                     for r in range(row_ref.shape[0])]
            return jnp.concatenate(lines, axis=0) + col_ref[...]

        cos = per_token(crow_ref, ccol_ref)
        sin = per_token(srow_ref, scol_ref)
        hpd = 4
        for grp in range(A_COLS // (hpd * HEAD_DIM)):
            c0 = grp * hpd * HEAD_DIM
            acc = jnp.dot(xn_ref[...], w_ref[:, c0:c0 + hpd * HEAD_DIM],
                          preferred_element_type=F32)
            for sub in range(hpd):
                hh = hpd * grp + sub
                blk = acc[:, sub * HEAD_DIM:(sub + 1) * HEAD_DIM]
                if hh < VA_COL:
                    gain = qn_ref[...] if hh < KA_COL else kn_ref[...]
                    y = blk * lax.rsqrt(_lane_sum_mxu(blk * blk) * (1.0 / HEAD_DIM) + NORM_EPS) * gain
                    blk = y * cos + pltpu.roll(y, HEAD_DIM // 2, 1) * sin
                o_ref[:, hh * HEAD_DIM:(hh + 1) * HEAD_DIM] = blk.astype(BF16)

    @pl.when(j == 1)
    def _():
        nq = B_HEADS * HEAD_DIM
        acc = jnp.dot(xn_ref[...], w_ref[...], preferred_element_type=F32)
        o_ref[:, :nq] = (acc[:, :nq] * (SM_SCALE * LOG2E)).astype(BF16)
        o_ref[:, nq:] = acc[:, nq:].astype(BF16)

    @pl.when(j > 1)
    def _():
        o_ref[...] = jnp.dot(xn_ref[...], w_ref[...], preferred_element_type=F32).astype(BF16)


def _inproj(x, g, w, rope, qn, kn, *, tm=1024):
    s_len, d = x.shape
    n = w.shape[1]
    tn = A_COLS
    assert s_len % tm == 0 and n % tn == 0 and tm % GRID_W == 0
    crow, srow, ccol, scol = rope
    tile_rows = tm // GRID_W
    ccol, scol = (jnp.tile(t, (tile_rows, 1)) for t in (ccol, scol))
    return pl.pallas_call(
        _inproj_kernel,
        grid=(s_len // tm, n // tn),
        in_specs=[
            pl.BlockSpec((tm, d), lambda i, j: (i, 0)),
            pl.BlockSpec((1, d), lambda i, j: (0, 0)),
            pl.BlockSpec((d, tn), lambda i, j: (0, j)),
            pl.BlockSpec((tile_rows, HEAD_DIM), lambda i, j: (i, 0)),
            pl.BlockSpec((tile_rows, HEAD_DIM), lambda i, j: (i, 0)),
            pl.BlockSpec((tm, HEAD_DIM), lambda i, j: (0, 0)),
            pl.BlockSpec((tm, HEAD_DIM), lambda i, j: (0, 0)),
            pl.BlockSpec((1, HEAD_DIM), lambda i, j: (0, 0)),
            pl.BlockSpec((1, HEAD_DIM), lambda i, j: (0, 0)),
        ],
        out_specs=pl.BlockSpec((tm, tn), lambda i, j: (i, j)),
        out_shape=jax.ShapeDtypeStruct((s_len, n), BF16),
        scratch_shapes=[pltpu.VMEM((tm, d), BF16)],
        compiler_params=pltpu.CompilerParams(
            dimension_semantics=("parallel", "arbitrary"), vmem_limit_bytes=VMEM_LIMIT),
        name="inproj",
    )(x, g, w, crow, srow, ccol, scol, qn, kn)


EXP2_SAFE_BOUND = 60.0


ONES_ROWS = 16


def _attn_a_kernel(*refs, tq, tk, nk, n_cast):
    q_ref, k_ref, v_ref = refs[:3]
    w32_refs = refs[3:3 + n_cast]
    o_ref = refs[3 + n_cast]
    w16_refs = refs[4 + n_cast:4 + 2 * n_cast]
    qt_ref, vxt_ref, kmax_ref, m_ref, acc_ref = refs[4 + 2 * n_cast:]
    i = pl.program_id(1)
    for w32, w16 in zip(w32_refs, w16_refs):
        w16[...] = w32[...].astype(BF16)

    @pl.when(i == 0)
    def _prepare_kv():
        row = lax.broadcasted_iota(jnp.int32, (ONES_ROWS, tk), 0)
        ones_rows = jnp.where(row == 0, 1.0, 0.0).astype(BF16)

        def prep(c, kmax):
            off = pl.multiple_of(c * tk, tk)
            vxt_ref[c, :HEAD_DIM, :] = v_ref[pl.ds(off, tk), :].astype(F32).T.astype(BF16)
            vxt_ref[c, HEAD_DIM:, :] = ones_rows
            kf = k_ref[pl.ds(off, tk), :].astype(F32)
            return jnp.maximum(kmax, jnp.sum(kf * kf, axis=1, keepdims=True))

        kmax = lax.fori_loop(0, nk, prep, jnp.zeros((tk, 1), F32))
        kmax_ref[0] = jnp.max(kmax)

    for h in range(A_GROUP):
        qt_ref[:, h * tq:(h + 1) * tq] = (
            q_ref[:, h * HEAD_DIM:(h + 1) * HEAD_DIM].astype(F32).T.astype(BF16))
    acc_ref[...] = jnp.zeros(acc_ref.shape, F32)
    qf = qt_ref[...].astype(F32)
    qmax = jnp.max(jnp.sum(qf * qf, axis=0, keepdims=True))
    no_max_needed = qmax * kmax_ref[0] <= EXP2_SAFE_BOUND * EXP2_SAFE_BOUND

    @pl.when(no_max_needed)
    def _plain():
        def body(j, carry):
            off = pl.multiple_of(j * tk, tk)
            s = jnp.dot(k_ref[pl.ds(off, tk), :], qt_ref[...], preferred_element_type=F32)
            p = jnp.exp2(s).astype(BF16)
            acc_ref[...] += jnp.dot(vxt_ref[j], p, preferred_element_type=F32)
            return carry

        lax.fori_loop(0, nk, body, 0, unroll=16)

    @pl.when(jnp.logical_not(no_max_needed))
    def _online():
        m_ref[...] = jnp.full(m_ref.shape, -jnp.inf, F32)

        def body(j, carry):
            off = pl.multiple_of(j * tk, tk)
            s = jnp.dot(k_ref[pl.ds(off, tk), :], qt_ref[...], preferred_element_type=F32)
            m_prev = m_ref[...]
            m_new = jnp.maximum(m_prev, jnp.max(s, axis=0, keepdims=True))
            p = jnp.exp2(s - m_new).astype(BF16)
            acc_ref[...] = jnp.exp2(m_prev - m_new) * acc_ref[...] + jnp.dot(
                vxt_ref[j], p, preferred_element_type=F32)
            m_ref[...] = m_new
            return carry

        lax.fori_loop(0, nk, body, 0)

    out_t = acc_ref[:HEAD_DIM, :] / acc_ref[HEAD_DIM:HEAD_DIM + 1, :]
    for h in range(A_GROUP):
        o_ref[:, h * HEAD_DIM:(h + 1) * HEAD_DIM] = out_t[:, h * tq:(h + 1) * tq].T.astype(BF16)


def _attn_a(proj, cast_ws, *, tq=512, tk=512):
    s_len = proj.shape[0]
    assert s_len % tq == 0 and s_len % tk == 0
    m_cols = A_GROUP * tq
    gw = A_GROUP * HEAD_DIM
    nk = s_len // tk
    n_i = s_len // tq
    n_steps = A_KV_HEADS * n_i
    bf16_rows = 16
    slabs = []
    for w in cast_ws:
        assert w.shape[0] % (n_steps * bf16_rows) == 0
        slabs.append(pl.BlockSpec((w.shape[0] // n_steps, w.shape[1]),
                                  lambda g, i: (g * n_i + i, 0)))
    kern = functools.partial(_attn_a_kernel, tq=tq, tk=tk, nk=nk, n_cast=len(cast_ws))
    outs = pl.pallas_call(
        kern,
        grid=(A_KV_HEADS, n_i),
        in_specs=[
            pl.BlockSpec((tq, gw), lambda g, i: (i, g)),
            pl.BlockSpec((s_len, HEAD_DIM), lambda g, i: (0, KA_COL + g)),
            pl.BlockSpec((s_len, HEAD_DIM), lambda g, i: (0, VA_COL + g)),
        ] + slabs,
        out_specs=[pl.BlockSpec((tq, gw), lambda g, i: (i, g))] + slabs,
        out_shape=[jax.ShapeDtypeStruct((s_len, A_Q_HEADS * HEAD_DIM), BF16)]
        + [jax.ShapeDtypeStruct(w.shape, BF16) for w in cast_ws],
        scratch_shapes=[
            pltpu.VMEM((HEAD_DIM, m_cols), BF16),
            pltpu.VMEM((nk, HEAD_DIM + ONES_ROWS, tk), BF16),
            pltpu.SMEM((1,), F32),
            pltpu.VMEM((1, m_cols), F32),
            pltpu.VMEM((HEAD_DIM + ONES_ROWS, m_cols), F32),
        ],
        compiler_params=pltpu.CompilerParams(
            dimension_semantics=("arbitrary", "arbitrary"), vmem_limit_bytes=VMEM_LIMIT),
        name="attn_global",
    )(proj, proj, proj, *cast_ws)
    return outs[0], tuple(outs[1:])


def _nb_block_types(rows):
    last_q = rows - NB_QROWS
    return ((0, 0), (2 * NB_QROWS, 2 * NB_QROWS - WIN_ROWS // 2), (last_q, rows - NB_KROWS))


def _attn_b_kernel(rpb_ref, q_ref, k_ref, v_ref, o_ref, tcol_ref, tab_ref, vx_ref, *, rows, ub):
    h = pl.program_id(0)
    step = pl.program_id(1)
    nblk = rows // NB_QROWS
    n_dr = 2 * WIN_ROWS - 1
    n_dc = 2 * WIN_COLS - 1

    @pl.when(step == 0)
    def _build_tables():
        chunk = NB_TQ
        lane_v = lax.broadcasted_iota(jnp.int32, (chunk, HEAD_DIM), 1)
        ones_col = jnp.where(lane_v == 0, 1.0, 0.0).astype(BF16)

        def copy_v(c, carry):
            off_c = pl.multiple_of(c * chunk, chunk)
            vx_ref[pl.ds(off_c, chunk), :HEAD_DIM] = v_ref[pl.ds(off_c, chunk), :]
            vx_ref[pl.ds(off_c, chunk), HEAD_DIM:] = ones_col
            return carry

        lax.fori_loop(0, v_ref.shape[0] // chunk, copy_v, 0)

        c = lax.broadcasted_iota(jnp.int32, (GRID_W, 2 * GRID_W), 0)
        kc = lax.broadcasted_iota(jnp.int32, (GRID_W, 2 * GRID_W), 1) % GRID_W
        c0 = jnp.clip(c - WIN_COLS // 2, 0, GRID_W - WIN_COLS)
        col_ok = (kc >= c0) & (kc < c0 + WIN_COLS)
        dc = kc - c + (WIN_COLS - 1)
        base = h * (n_dr * n_dc)

        def row_body(a, carry):
            t = jnp.full((GRID_W, 2 * GRID_W), NEG_BIG, F32)
            for b in range(n_dc):
                t = jnp.where(dc == b, rpb_ref[base + a * n_dc + b] * LOG2E, t)
            tcol_ref[a] = jnp.where(col_ok, t, NEG_BIG)
            return carry

        lax.fori_loop(0, n_dr, row_body, 0)

        lane = lax.broadcasted_iota(jnp.int32, (GRID_W, 2 * GRID_W), 1)
        neg = jnp.full((GRID_W, 2 * GRID_W), NEG_BIG, F32)
        for t, (r_first, k_first) in enumerate(_nb_block_types(rows)):
            for qr in range(NB_QROWS):
                r = r_first + qr
                r0 = min(max(r - WIN_ROWS // 2, 0), rows - WIN_ROWS)
                for jj in range(NB_KROWS // 2):
                    halves = []
                    for kr in (2 * jj, 2 * jj + 1):
                        k_abs = k_first + kr
                        ok = r0 <= k_abs < r0 + WIN_ROWS
                        halves.append(tcol_ref[k_abs - r + WIN_ROWS - 1] if ok else neg)
                    tab_ref[t, qr * GRID_W:(qr + 1) * GRID_W,
                            jj * 2 * GRID_W:(jj + 1) * 2 * GRID_W] = jnp.where(
                                lane < GRID_W, halves[0], halves[1])

    for u in range(ub):
        pb = step * ub + u
        btype = jnp.where(pb == 0, 0, jnp.where(pb == nblk - 1, 2, 1))
        k_first = jnp.clip(pb * NB_QROWS - WIN_ROWS // 2, 0, rows - NB_KROWS)
        off = pl.multiple_of(k_first * GRID_W, GRID_W)
        ks = k_ref[pl.ds(off, NB_TK), :]
        s = lax.dot_general(q_ref[u * NB_TQ:(u + 1) * NB_TQ, :], ks, (((1,), (1,)), ((), ())),
                            preferred_element_type=F32)
        s = s + tab_ref[btype]
        m = jnp.max(s, axis=1, keepdims=True)
        p = jnp.exp2(s - m).astype(BF16)
        acc = jnp.dot(p, vx_ref[pl.ds(off, NB_TK), :], preferred_element_type=F32)
        o_ref[u * NB_TQ:(u + 1) * NB_TQ, :] = (
            acc[:, :HEAD_DIM] / acc[:, HEAD_DIM:HEAD_DIM + 1]).astype(BF16)


def _attn_b(proj, rpb_flat, *, max_ub=16):
    s_len = proj.shape[0]
    rows = s_len // GRID_W
    nblk = rows // NB_QROWS
    assert rows % NB_QROWS == 0 and rows >= 3 * NB_QROWS + WIN_ROWS
    ub = max(u for u in range(1, max_ub + 1) if nblk % u == 0)
    kern = functools.partial(_attn_b_kernel, rows=rows, ub=ub)
    return pl.pallas_call(
        kern,
        grid=(B_HEADS, nblk // ub),
        in_specs=[
            pl.BlockSpec(memory_space=pltpu.SMEM),
            pl.BlockSpec((ub * NB_TQ, HEAD_DIM), lambda h, p: (p, QB_COL + h)),
            pl.BlockSpec((s_len, HEAD_DIM), lambda h, p: (0, KB_COL + h)),
            pl.BlockSpec((s_len, HEAD_DIM), lambda h, p: (0, VB_COL + h)),
        ],
        out_specs=pl.BlockSpec((ub * NB_TQ, HEAD_DIM), lambda h, p: (p, h)),
        out_shape=jax.ShapeDtypeStruct((s_len, B_HEADS * HEAD_DIM), BF16),
        scratch_shapes=[
            pltpu.VMEM((2 * WIN_ROWS - 1, GRID_W, 2 * GRID_W), F32),
            pltpu.VMEM((3, NB_TQ, NB_TK), F32),
            pltpu.VMEM((s_len, 2 * HEAD_DIM), BF16),
        ],
        compiler_params=pltpu.CompilerParams(
            dimension_semantics=("parallel", "arbitrary"), vmem_limit_bytes=VMEM_LIMIT),
        name="attn_nbr",
    )(rpb_flat, proj, proj, proj)


ROW_SUB = 256


def _row_subblocks(tm):
    return [slice(r, r + ROW_SUB) for r in range(0, tm, ROW_SUB)]


def _oproj_kernel(oa_ref, ob_ref, wo_ref, x_ref, g_ref, h_ref):
    na = oa_ref.shape[1]
    for rows in _row_subblocks(x_ref.shape[0]):
        mix = jnp.dot(oa_ref[rows, :], wo_ref[:na, :], preferred_element_type=F32)
        mix = mix + jnp.dot(ob_ref[rows, :], wo_ref[na:, :], preferred_element_type=F32)
        h_ref[rows, :] = x_ref[rows, :] + _rms(mix, g_ref[...])


def _oproj(oa, ob, wo, x, g, *, tm=512):
    s_len, d = x.shape
    assert s_len % tm == 0 and tm % ROW_SUB == 0
    return pl.pallas_call(
        _oproj_kernel,
        grid=(s_len // tm,),
        in_specs=[
            pl.BlockSpec((tm, oa.shape[1]), lambda i: (i, 0)),
            pl.BlockSpec((tm, ob.shape[1]), lambda i: (i, 0)),
            pl.BlockSpec(wo.shape, lambda i: (0, 0), pipeline_mode=pl.Buffered(1)),
            pl.BlockSpec((tm, d), lambda i: (i, 0)),
            pl.BlockSpec((1, d), lambda i: (0, 0)),
        ],
        out_specs=pl.BlockSpec((tm, d), lambda i: (i, 0)),
        out_shape=jax.ShapeDtypeStruct((s_len, d), F32),
        compiler_params=pltpu.CompilerParams(
            dimension_semantics=("parallel",), vmem_limit_bytes=VMEM_LIMIT),
        name="oproj",
    )(oa, ob, wo, x, g)


def _mlp_kernel(h_ref, gpre_ref, wup_ref, wdn_ref, gpost_ref, o_ref, xn_ref, acc_ref):
    f = pl.program_id(1)

    def ff_chunk():
        u = jnp.dot(xn_ref[...], wup_ref[...], preferred_element_type=F32)
        a = jnp.square(jnp.maximum(u, 0.0)).astype(BF16)
        return jnp.dot(a, wdn_ref[...], preferred_element_type=F32)

    @pl.when(f == 0)
    def _():
        xn_ref[...] = _rms(h_ref[...], gpre_ref[...]).astype(BF16)
        acc_ref[...] = ff_chunk()

    @pl.when(f > 0)
    def _():
        acc_ref[...] += ff_chunk()

    @pl.when(f == pl.num_programs(1) - 1)
    def _():
        o_ref[...] = h_ref[...] + _rms(acc_ref[...], gpost_ref[...])


def _mlp(h, gpre, wup, wdn, gpost, *, tm=512, tf=1024):
    s_len, d = h.shape
    d_ff = wup.shape[1]
    assert s_len % tm == 0 and d_ff % tf == 0
    return pl.pallas_call(
        _mlp_kernel,
        grid=(s_len // tm, d_ff // tf),
        in_specs=[
            pl.BlockSpec((tm, d), lambda i, f: (i, 0)),
            pl.BlockSpec((1, d), lambda i, f: (0, 0)),
            pl.BlockSpec((d, tf), lambda i, f: (0, f)),
            pl.BlockSpec((tf, d), lambda i, f: (f, 0)),
            pl.BlockSpec((1, d), lambda i, f: (0, 0)),
        ],
        out_specs=pl.BlockSpec((tm, d), lambda i, f: (i, 0)),
        out_shape=jax.ShapeDtypeStruct((s_len, d), F32),
        scratch_shapes=[pltpu.VMEM((tm, d), BF16), pltpu.VMEM((tm, d), F32)],
        compiler_params=pltpu.CompilerParams(
            dimension_semantics=("parallel", "arbitrary"), vmem_limit_bytes=VMEM_LIMIT),
        name="mlp",
    )(h, gpre, wup, wdn, gpost)


def _ple_kernel(h_ref, p_ref, gpre_ref, wg_ref, wp_ref, gpost_ref, o_ref):
    for rows in _row_subblocks(h_ref.shape[0]):
        h = h_ref[rows, :]
        xn = _rms(h, gpre_ref[...]).astype(BF16)
        gate = jax.nn.sigmoid(jnp.dot(xn, wg_ref[...], preferred_element_type=F32))
        e = jnp.dot(p_ref[rows, :].astype(BF16), wp_ref[...], preferred_element_type=F32) * gate
        o_ref[rows, :] = h + _rms(e, gpost_ref[...])


def _ple(h, p, gpre, wg, wp, gpost, *, tm=512):
    s_len, d = h.shape
    dp = p.shape[1]
    assert s_len % tm == 0 and tm % ROW_SUB == 0
    return pl.pallas_call(
        _ple_kernel,
        grid=(s_len // tm,),
        in_specs=[
            pl.BlockSpec((tm, d), lambda i: (i, 0)),
            pl.BlockSpec((tm, dp), lambda i: (i, 0)),
            pl.BlockSpec((1, d), lambda i: (0, 0)),
            pl.BlockSpec(wg.shape, lambda i: (0, 0), pipeline_mode=pl.Buffered(1)),
            pl.BlockSpec(wp.shape, lambda i: (0, 0), pipeline_mode=pl.Buffered(1)),
            pl.BlockSpec((1, d), lambda i: (0, 0)),
        ],
        out_specs=pl.BlockSpec((tm, d), lambda i: (i, 0)),
        out_shape=jax.ShapeDtypeStruct((s_len, d), F32),
        compiler_params=pltpu.CompilerParams(
            dimension_semantics=("parallel",), vmem_limit_bytes=VMEM_LIMIT),
        name="ple",
    )(h, p, gpre, wg, wp, gpost)


def _pair_major(a):
    lead = a.shape[:-1]
    return a.reshape(lead + (2, 2, HEAD_DIM // 4)).swapaxes(-3, -2).reshape(lead + (HEAD_DIM,))


def _rope_tables(s_len):
    rows = s_len // GRID_W
    n_freq = HEAD_DIM // 4
    freqs = ROPE_THETA ** (-jnp.arange(n_freq, dtype=F32) / n_freq)
    ar = jnp.arange(rows, dtype=F32)[:, None] * freqs[None, :]
    ac = jnp.arange(GRID_W, dtype=F32)[:, None] * freqs[None, :]
    zr, zc = jnp.zeros_like(ar), jnp.zeros_like(ac)
    crow = jnp.concatenate([jnp.cos(ar), zr, jnp.cos(ar), zr], axis=-1)
    srow = jnp.concatenate([-jnp.sin(ar), zr, jnp.sin(ar), zr], axis=-1)
    ccol = jnp.concatenate([zc, jnp.cos(ac), zc, jnp.cos(ac)], axis=-1)
    scol = jnp.concatenate([zc, -jnp.sin(ac), zc, jnp.sin(ac)], axis=-1)
    return crow, srow, ccol, scol


def kernel(x, p, pre_mix_norm, w_in, q_norm, k_norm, rel_pos_bias, w_o, post_mix_norm,
           pre_mlp_norm, w_up, w_down, post_mlp_norm, pre_ple_norm, w_ple_gate, w_ple_proj,
           post_ple_norm):
    b, s_len, d = x.shape
    depth = w_in.shape[0]
    rope = _rope_tables(s_len)
    outs = []
    for bi in range(b):
        h = x[bi]
        for i in range(depth):
            qn = _pair_major(q_norm[i] * (SM_SCALE * LOG2E)).reshape(1, HEAD_DIM)
            kn = _pair_major(k_norm[i]).reshape(1, HEAD_DIM)
            n_rot = VA_COL * HEAD_DIM
            w_rot = _pair_major(w_in[i][:, :n_rot].reshape(d, VA_COL, HEAD_DIM)).reshape(d, n_rot)
            w_proj = jnp.concatenate([w_rot, w_in[i][:, n_rot:]], axis=1).astype(BF16)
            proj = _inproj(h, pre_mix_norm[i].reshape(1, d), w_proj, rope, qn, kn)
            out_a, (wo16, wup16, wdn16, wg16) = _attn_a(
                proj, (w_o[i], w_up[i], w_down[i], w_ple_gate[i]))
            out_b = _attn_b(proj, rel_pos_bias[i].reshape(-1))
            h = _oproj(out_a, out_b, wo16, h, post_mix_norm[i].reshape(1, d))
            h = _mlp(h, pre_mlp_norm[i].reshape(1, d), wup16, wdn16, post_mlp_norm[i].reshape(1, d))
            h = _ple(h, p[i, bi], pre_ple_norm[i].reshape(1, d), wg16,
                     w_ple_proj[i].astype(BF16), post_ple_norm[i].reshape(1, d))
        outs.append(h)
    return jnp.stack(outs, axis=0)
```

```python
import functools
import math

import jax
import jax.numpy as jnp
from jax import lax
from jax.experimental import pallas as pl
from jax.experimental.pallas import tpu as pltpu

F32 = jnp.float32
BF16 = jnp.bfloat16

HEAD_DIM = 128
GRID_W = 64
A_Q_HEADS = 8
A_KV_HEADS = 2
A_GROUP = A_Q_HEADS // A_KV_HEADS
B_HEADS = 8
WIN_ROWS = 8
WIN_COLS = 16
ROPE_THETA = 10000.0
NORM_EPS = 1e-6
LOG2E = math.log2(math.e)
SM_SCALE = 1.0 / math.sqrt(HEAD_DIM)
NEG_BIG = -1e30

QA_COL, KA_COL, VA_COL = 0, 8, 10
A_COLS = 12 * HEAD_DIM
QB_COL, KB_COL, VB_COL = 0, 8, 16

V7X_VMEM_BYTES = 64 * 1024 * 1024
VMEM_LIMIT = V7X_VMEM_BYTES - 8 * 1024 * 1024

NB_QROWS = 4
NB_KROWS = NB_QROWS + WIN_ROWS
NB_TQ = NB_QROWS * GRID_W
NB_TK = NB_KROWS * GRID_W


def _rms(x, g):
    return x * lax.rsqrt(jnp.mean(x * x, axis=-1, keepdims=True) + NORM_EPS) * g


def _inproj_a_kernel(x_ref, g_ref, wat_ref, crow_ref, srow_ref, ccol_ref, scol_ref, expand_ref,
                     qn_ref, kn_ref, oat_ref, xn_ref):
    tm = x_ref.shape[0]
    xn = _rms(x_ref[...], g_ref[...]).astype(BF16)
    xn_ref[...] = xn
    acc = lax.dot_general(wat_ref[...], xn, (((1,), (1,)), ((), ())),
                          preferred_element_type=F32)

    def per_token(row_ref, col_ref):
        return jnp.dot(row_ref[...], expand_ref[...], preferred_element_type=F32) + col_ref[...]

    cos = per_token(crow_ref, ccol_ref)
    sin = per_token(srow_ref, scol_ref)
    half = HEAD_DIM // 2
    reps = tm // HEAD_DIM
    for hh in range(A_COLS // HEAD_DIM):
        blk = acc[hh * HEAD_DIM:(hh + 1) * HEAD_DIM, :]
        if hh < VA_COL:
            gain = jnp.tile(qn_ref[...] if hh < KA_COL else kn_ref[...], (1, reps))
            ssq = jnp.sum(blk * blk, axis=0, keepdims=True)
            y = blk * lax.rsqrt(ssq * (1.0 / HEAD_DIM) + NORM_EPS) * gain
            partner = jnp.concatenate([y[half:, :], y[:half, :]], axis=0)
            blk = y * cos + partner * sin
        oat_ref[hh * HEAD_DIM:(hh + 1) * HEAD_DIM, :] = blk.astype(BF16)


def _inproj_a(x, g, wat, rope, qn, kn, *, tm=1024):
    s_len, d = x.shape
    assert s_len % tm == 0 and tm % GRID_W == 0 and wat.shape == (A_COLS, d)
    crow, srow, ccol, scol, expand = rope
    tile_rows = tm // GRID_W
    const = lambda i: (0, 0)
    return pl.pallas_call(
        _inproj_a_kernel,
        grid=(s_len // tm,),
        in_specs=[
            pl.BlockSpec((tm, d), lambda i: (i, 0)),
            pl.BlockSpec((1, d), const),
            pl.BlockSpec(wat.shape, const, pipeline_mode=pl.Buffered(1)),
            pl.BlockSpec((None, HEAD_DIM, 2 * tile_rows), lambda i: (i, 0, 0)),
            pl.BlockSpec((None, HEAD_DIM, 2 * tile_rows), lambda i: (i, 0, 0)),
            pl.BlockSpec((HEAD_DIM, tm), const),
            pl.BlockSpec((HEAD_DIM, tm), const),
            pl.BlockSpec((2 * tile_rows, tm), const),
            pl.BlockSpec((HEAD_DIM, HEAD_DIM), const),
            pl.BlockSpec((HEAD_DIM, HEAD_DIM), const),
        ],
        out_specs=[pl.BlockSpec((A_COLS, tm), lambda i: (0, i)),
                   pl.BlockSpec((tm, d), lambda i: (i, 0))],
        out_shape=[jax.ShapeDtypeStruct((A_COLS, s_len), BF16),
                   jax.ShapeDtypeStruct((s_len, d), BF16)],
        compiler_params=pltpu.CompilerParams(
            dimension_semantics=("parallel",), vmem_limit_bytes=VMEM_LIMIT),
        name="inproj_a",
    )(x, g, wat, crow, srow, ccol, scol, expand, qn, kn)


def _inproj_b_kernel(xn_ref, w_ref, o_ref):
    j = pl.program_id(1)

    @pl.when(j == 0)
    def _():
        nq = B_HEADS * HEAD_DIM
        acc = jnp.dot(xn_ref[...], w_ref[...], preferred_element_type=F32)
        o_ref[:, :nq] = (acc[:, :nq] * (SM_SCALE * LOG2E)).astype(BF16)
        o_ref[:, nq:] = acc[:, nq:].astype(BF16)

    @pl.when(j > 0)
    def _():
        o_ref[...] = jnp.dot(xn_ref[...], w_ref[...], preferred_element_type=F32).astype(BF16)


def _inproj_b(xn, w, *, tm=1024, tn=A_COLS):
    s_len, d = xn.shape
    n = w.shape[1]
    assert s_len % tm == 0 and n % tn == 0 and tn >= B_HEADS * HEAD_DIM
    return pl.pallas_call(
        _inproj_b_kernel,
        grid=(s_len // tm, n // tn),
        in_specs=[
            pl.BlockSpec((tm, d), lambda i, j: (i, 0)),
            pl.BlockSpec((d, tn), lambda i, j: (0, j)),
        ],
        out_specs=pl.BlockSpec((tm, tn), lambda i, j: (i, j)),
        out_shape=jax.ShapeDtypeStruct((s_len, n), BF16),
        compiler_params=pltpu.CompilerParams(
            dimension_semantics=("parallel", "arbitrary"), vmem_limit_bytes=VMEM_LIMIT),
        name="inproj_b",
    )(xn, w)


EXP2_SAFE_BOUND = 60.0


ONES_ROWS = 16


def _attn_a_kernel(*refs, tq, tk, nk, n_cast):
    q_ref, kt_ref, vt_ref = refs[:3]
    w32_refs = refs[3:3 + n_cast]
    o_ref = refs[3 + n_cast]
    w16_refs = refs[4 + n_cast:4 + 2 * n_cast]
    qt_ref, kn_ref, vxt_ref, kmax_ref, m_ref, acc_ref = refs[4 + 2 * n_cast:]
    i = pl.program_id(1)
    for w32, w16 in zip(w32_refs, w16_refs):
        w16[...] = w32[...].astype(BF16)

    @pl.when(i == 0)
    def _prepare_kv():
        row = lax.broadcasted_iota(jnp.int32, (ONES_ROWS, tk), 0)
        ones_rows = jnp.where(row == 0, 1.0, 0.0).astype(BF16)
        kmax = jnp.zeros((1, tk), F32)
        for c in range(nk):
            cols = slice(c * tk, (c + 1) * tk)
            ktf = kt_ref[:, cols].astype(F32)
            kn_ref[cols, :] = ktf.T.astype(BF16)
            vxt_ref[c, :HEAD_DIM, :] = vt_ref[:, cols]
            vxt_ref[c, HEAD_DIM:, :] = ones_rows
            kmax = jnp.maximum(kmax, jnp.sum(ktf * ktf, axis=0, keepdims=True))
        kmax_ref[0] = jnp.max(kmax)

    for h in range(A_GROUP):
        qt_ref[:, h * tq:(h + 1) * tq] = q_ref[h * HEAD_DIM:(h + 1) * HEAD_DIM, :]
    acc_ref[...] = jnp.zeros(acc_ref.shape, F32)
    qf = qt_ref[...].astype(F32)
    qmax = jnp.max(jnp.sum(qf * qf, axis=0, keepdims=True))
    no_max_needed = qmax * kmax_ref[0] <= EXP2_SAFE_BOUND * EXP2_SAFE_BOUND

    @pl.when(no_max_needed)
    def _plain():
        def body(j, carry):
            off = pl.multiple_of(j * tk, tk)
            s = jnp.dot(kn_ref[pl.ds(off, tk), :], qt_ref[...], preferred_element_type=F32)
            p = jnp.exp2(s).astype(BF16)
            acc_ref[...] += jnp.dot(vxt_ref[j], p, preferred_element_type=F32)
            return carry

        lax.fori_loop(0, nk, body, 0, unroll=16)

    @pl.when(jnp.logical_not(no_max_needed))
    def _online():
        m_ref[...] = jnp.full(m_ref.shape, -jnp.inf, F32)

        def body(j, carry):
            off = pl.multiple_of(j * tk, tk)
            s = jnp.dot(kn_ref[pl.ds(off, tk), :], qt_ref[...], preferred_element_type=F32)
            m_prev = m_ref[...]
            m_new = jnp.maximum(m_prev, jnp.max(s, axis=0, keepdims=True))
            p = jnp.exp2(s - m_new).astype(BF16)
            acc_ref[...] = jnp.exp2(m_prev - m_new) * acc_ref[...] + jnp.dot(
                vxt_ref[j], p, preferred_element_type=F32)
            m_ref[...] = m_new
            return carry

        lax.fori_loop(0, nk, body, 0)

    out_t = acc_ref[:HEAD_DIM, :] / acc_ref[HEAD_DIM:HEAD_DIM + 1, :]
    for h in range(A_GROUP):
        o_ref[:, h * HEAD_DIM:(h + 1) * HEAD_DIM] = out_t[:, h * tq:(h + 1) * tq].T.astype(BF16)


def _attn_a(proj_t, cast_ws, *, tq=512, tk=512):
    s_len = proj_t.shape[1]
    assert s_len % tq == 0 and s_len % tk == 0
    m_cols = A_GROUP * tq
    gw = A_GROUP * HEAD_DIM
    nk = s_len // tk
    n_i = s_len // tq
    n_steps = A_KV_HEADS * n_i
    bf16_rows = 16
    slabs = []
    for w in cast_ws:
        assert w.shape[0] % (n_steps * bf16_rows) == 0
        slabs.append(pl.BlockSpec((w.shape[0] // n_steps, w.shape[1]),
                                  lambda g, i: (g * n_i + i, 0)))
    kern = functools.partial(_attn_a_kernel, tq=tq, tk=tk, nk=nk, n_cast=len(cast_ws))
    outs = pl.pallas_call(
        kern,
        grid=(A_KV_HEADS, n_i),
        in_specs=[
            pl.BlockSpec((gw, tq), lambda g, i: (g, i)),
            pl.BlockSpec((HEAD_DIM, s_len), lambda g, i: (KA_COL + g, 0)),
            pl.BlockSpec((HEAD_DIM, s_len), lambda g, i: (VA_COL + g, 0)),
        ] + slabs,
        out_specs=[pl.BlockSpec((tq, gw), lambda g, i: (i, g))] + slabs,
        out_shape=[jax.ShapeDtypeStruct((s_len, A_Q_HEADS * HEAD_DIM), BF16)]
        + [jax.ShapeDtypeStruct(w.shape, BF16) for w in cast_ws],
        scratch_shapes=[
            pltpu.VMEM((HEAD_DIM, m_cols), BF16),
            pltpu.VMEM((s_len, HEAD_DIM), BF16),
            pltpu.VMEM((nk, HEAD_DIM + ONES_ROWS, tk), BF16),
            pltpu.SMEM((1,), F32),
            pltpu.VMEM((1, m_cols), F32),
            pltpu.VMEM((HEAD_DIM + ONES_ROWS, m_cols), F32),
        ],
        compiler_params=pltpu.CompilerParams(
            dimension_semantics=("arbitrary", "arbitrary"), vmem_limit_bytes=VMEM_LIMIT),
        name="attn_global",
    )(proj_t, proj_t, proj_t, *cast_ws)
    return outs[0], tuple(outs[1:])


def _nb_block_types(rows):
    last_q = rows - NB_QROWS
    return ((0, 0), (2 * NB_QROWS, 2 * NB_QROWS - WIN_ROWS // 2), (last_q, rows - NB_KROWS))


def _attn_b_kernel(rpb_ref, q_ref, k_ref, v_ref, o_ref, tcol_ref, tab_ref, vx_ref, *, rows, ub):
    h = pl.program_id(0)
    step = pl.program_id(1)
    nblk = rows // NB_QROWS
    n_dr = 2 * WIN_ROWS - 1
    n_dc = 2 * WIN_COLS - 1

    @pl.when(step == 0)
    def _build_tables():
        chunk = NB_TQ
        lane_v = lax.broadcasted_iota(jnp.int32, (chunk, HEAD_DIM), 1)
        ones_col = jnp.where(lane_v == 0, 1.0, 0.0).astype(BF16)

        def copy_v(c, carry):
            off_c = pl.multiple_of(c * chunk, chunk)
            vx_ref[pl.ds(off_c, chunk), :HEAD_DIM] = v_ref[pl.ds(off_c, chunk), :]
            vx_ref[pl.ds(off_c, chunk), HEAD_DIM:] = ones_col
            return carry

        lax.fori_loop(0, v_ref.shape[0] // chunk, copy_v, 0)

        c = lax.broadcasted_iota(jnp.int32, (GRID_W, 2 * GRID_W), 0)
        kc = lax.broadcasted_iota(jnp.int32, (GRID_W, 2 * GRID_W), 1) % GRID_W
        c0 = jnp.clip(c - WIN_COLS // 2, 0, GRID_W - WIN_COLS)
        col_ok = (kc >= c0) & (kc < c0 + WIN_COLS)
        dc = kc - c + (WIN_COLS - 1)
        base = h * (n_dr * n_dc)

        def row_body(a, carry):
            t = jnp.full((GRID_W, 2 * GRID_W), NEG_BIG, F32)
            for b in range(n_dc):
                t = jnp.where(dc == b, rpb_ref[base + a * n_dc + b] * LOG2E, t)
            tcol_ref[a] = jnp.where(col_ok, t, NEG_BIG)
            return carry

        lax.fori_loop(0, n_dr, row_body, 0)

        lane = lax.broadcasted_iota(jnp.int32, (GRID_W, 2 * GRID_W), 1)
        neg = jnp.full((GRID_W, 2 * GRID_W), NEG_BIG, F32)
        for t, (r_first, k_first) in enumerate(_nb_block_types(rows)):
            for qr in range(NB_QROWS):
                r = r_first + qr
                r0 = min(max(r - WIN_ROWS // 2, 0), rows - WIN_ROWS)
                for jj in range(NB_KROWS // 2):
                    halves = []
                    for kr in (2 * jj, 2 * jj + 1):
                        k_abs = k_first + kr
                        ok = r0 <= k_abs < r0 + WIN_ROWS
                        halves.append(tcol_ref[k_abs - r + WIN_ROWS - 1] if ok else neg)
                    tab_ref[t, qr * GRID_W:(qr + 1) * GRID_W,
                            jj * 2 * GRID_W:(jj + 1) * 2 * GRID_W] = jnp.where(
                                lane < GRID_W, halves[0], halves[1])

    for u in range(ub):
        pb = step * ub + u
        btype = jnp.where(pb == 0, 0, jnp.where(pb == nblk - 1, 2, 1))
        k_first = jnp.clip(pb * NB_QROWS - WIN_ROWS // 2, 0, rows - NB_KROWS)
        off = pl.multiple_of(k_first * GRID_W, GRID_W)
        ks = k_ref[pl.ds(off, NB_TK), :]
        s = lax.dot_general(q_ref[u * NB_TQ:(u + 1) * NB_TQ, :], ks, (((1,), (1,)), ((), ())),
                            preferred_element_type=F32)
        s = s + tab_ref[btype]
        m = jnp.max(s, axis=1, keepdims=True)
        p = jnp.exp2(s - m).astype(BF16)
        acc = jnp.dot(p, vx_ref[pl.ds(off, NB_TK), :], preferred_element_type=F32)
        o_ref[u * NB_TQ:(u + 1) * NB_TQ, :] = (
            acc[:, :HEAD_DIM] / acc[:, HEAD_DIM:HEAD_DIM + 1]).astype(BF16)


def _attn_b(proj, rpb_flat, *, max_ub=16):
    s_len = proj.shape[0]
    rows = s_len // GRID_W
    nblk = rows // NB_QROWS
    assert rows % NB_QROWS == 0 and rows >= 3 * NB_QROWS + WIN_ROWS
    ub = max(u for u in range(1, max_ub + 1) if nblk % u == 0)
    kern = functools.partial(_attn_b_kernel, rows=rows, ub=ub)
    return pl.pallas_call(
        kern,
        grid=(B_HEADS, nblk // ub),
        in_specs=[
            pl.BlockSpec(memory_space=pltpu.SMEM),
            pl.BlockSpec((ub * NB_TQ, HEAD_DIM), lambda h, p: (p, QB_COL + h)),
            pl.BlockSpec((s_len, HEAD_DIM), lambda h, p: (0, KB_COL + h)),
            pl.BlockSpec((s_len, HEAD_DIM), lambda h, p: (0, VB_COL + h)),
        ],
        out_specs=pl.BlockSpec((ub * NB_TQ, HEAD_DIM), lambda h, p: (p, h)),
        out_shape=jax.ShapeDtypeStruct((s_len, B_HEADS * HEAD_DIM), BF16),
        scratch_shapes=[
            pltpu.VMEM((2 * WIN_ROWS - 1, GRID_W, 2 * GRID_W), F32),
            pltpu.VMEM((3, NB_TQ, NB_TK), F32),
            pltpu.VMEM((s_len, 2 * HEAD_DIM), BF16),
        ],
        compiler_params=pltpu.CompilerParams(
            dimension_semantics=("parallel", "arbitrary"), vmem_limit_bytes=VMEM_LIMIT),
        name="attn_nbr",
    )(rpb_flat, proj, proj, proj)


ROW_SUB = 256


def _row_subblocks(tm):
    return [slice(r, r + ROW_SUB) for r in range(0, tm, ROW_SUB)]


def _oproj_kernel(oa_ref, ob_ref, wo_ref, x_ref, g_ref, h_ref):
    na = oa_ref.shape[1]
    for rows in _row_subblocks(x_ref.shape[0]):
        mix = jnp.dot(oa_ref[rows, :], wo_ref[:na, :], preferred_element_type=F32)
        mix = mix + jnp.dot(ob_ref[rows, :], wo_ref[na:, :], preferred_element_type=F32)
        h_ref[rows, :] = x_ref[rows, :] + _rms(mix, g_ref[...])


def _oproj(oa, ob, wo, x, g, *, tm=512):
    s_len, d = x.shape
    assert s_len % tm == 0 and tm % ROW_SUB == 0
    return pl.pallas_call(
        _oproj_kernel,
        grid=(s_len // tm,),
        in_specs=[
            pl.BlockSpec((tm, oa.shape[1]), lambda i: (i, 0)),
            pl.BlockSpec((tm, ob.shape[1]), lambda i: (i, 0)),
            pl.BlockSpec(wo.shape, lambda i: (0, 0), pipeline_mode=pl.Buffered(1)),
            pl.BlockSpec((tm, d), lambda i: (i, 0)),
            pl.BlockSpec((1, d), lambda i: (0, 0)),
        ],
        out_specs=pl.BlockSpec((tm, d), lambda i: (i, 0)),
        out_shape=jax.ShapeDtypeStruct((s_len, d), F32),
        compiler_params=pltpu.CompilerParams(
            dimension_semantics=("parallel",), vmem_limit_bytes=VMEM_LIMIT),
        name="oproj",
    )(oa, ob, wo, x, g)


def _mlp_kernel(h_ref, gpre_ref, wup_ref, wdn_ref, gpost_ref, o_ref, xn_ref, acc_ref):
    f = pl.program_id(1)

    def ff_chunk():
        u = jnp.dot(xn_ref[...], wup_ref[...], preferred_element_type=F32)
        a = jnp.square(jnp.maximum(u, 0.0)).astype(BF16)
        return jnp.dot(a, wdn_ref[...], preferred_element_type=F32)

    @pl.when(f == 0)
    def _():
        xn_ref[...] = _rms(h_ref[...], gpre_ref[...]).astype(BF16)
        acc_ref[...] = ff_chunk()

    @pl.when(f > 0)
    def _():
        acc_ref[...] += ff_chunk()

    @pl.when(f == pl.num_programs(1) - 1)
    def _():
        o_ref[...] = h_ref[...] + _rms(acc_ref[...], gpost_ref[...])


def _mlp(h, gpre, wup, wdn, gpost, *, tm=512, tf=1024):
    s_len, d = h.shape
    d_ff = wup.shape[1]
    assert s_len % tm == 0 and d_ff % tf == 0
    return pl.pallas_call(
        _mlp_kernel,
        grid=(s_len // tm, d_ff // tf),
        in_specs=[
            pl.BlockSpec((tm, d), lambda i, f: (i, 0)),
            pl.BlockSpec((1, d), lambda i, f: (0, 0)),
            pl.BlockSpec((d, tf), lambda i, f: (0, f)),
            pl.BlockSpec((tf, d), lambda i, f: (f, 0)),
            pl.BlockSpec((1, d), lambda i, f: (0, 0)),
        ],
        out_specs=pl.BlockSpec((tm, d), lambda i, f: (i, 0)),
        out_shape=jax.ShapeDtypeStruct((s_len, d), F32),
        scratch_shapes=[pltpu.VMEM((tm, d), BF16), pltpu.VMEM((tm, d), F32)],
        compiler_params=pltpu.CompilerParams(
            dimension_semantics=("parallel", "arbitrary"), vmem_limit_bytes=VMEM_LIMIT),
        name="mlp",
    )(h, gpre, wup, wdn, gpost)


def _ple_kernel(h_ref, p_ref, gpre_ref, wg_ref, wp_ref, gpost_ref, o_ref):
    for rows in _row_subblocks(h_ref.shape[0]):
        h = h_ref[rows, :]
        xn = _rms(h, gpre_ref[...]).astype(BF16)
        gate = jax.nn.sigmoid(jnp.dot(xn, wg_ref[...], preferred_element_type=F32))
        e = jnp.dot(p_ref[rows, :].astype(BF16), wp_ref[...], preferred_element_type=F32) * gate
        o_ref[rows, :] = h + _rms(e, gpost_ref[...])


def _ple(h, p, gpre, wg, wp, gpost, *, tm=512):
    s_len, d = h.shape
    dp = p.shape[1]
    assert s_len % tm == 0 and tm % ROW_SUB == 0
    return pl.pallas_call(
        _ple_kernel,
        grid=(s_len // tm,),
        in_specs=[
            pl.BlockSpec((tm, d), lambda i: (i, 0)),
            pl.BlockSpec((tm, dp), lambda i: (i, 0)),
            pl.BlockSpec((1, d), lambda i: (0, 0)),
            pl.BlockSpec(wg.shape, lambda i: (0, 0), pipeline_mode=pl.Buffered(1)),
            pl.BlockSpec(wp.shape, lambda i: (0, 0), pipeline_mode=pl.Buffered(1)),
            pl.BlockSpec((1, d), lambda i: (0, 0)),
        ],
        out_specs=pl.BlockSpec((tm, d), lambda i: (i, 0)),
        out_shape=jax.ShapeDtypeStruct((s_len, d), F32),
        compiler_params=pltpu.CompilerParams(
            dimension_semantics=("parallel",), vmem_limit_bytes=VMEM_LIMIT),
        name="ple",
    )(h, p, gpre, wg, wp, gpost)


def _pair_major(a):
    lead = a.shape[:-1]
    return a.reshape(lead + (2, 2, HEAD_DIM // 4)).swapaxes(-3, -2).reshape(lead + (HEAD_DIM,))


def _rope_tables(s_len, tm):
    rows = s_len // GRID_W
    tile_rows = tm // GRID_W
    n_freq = HEAD_DIM // 4
    freqs = ROPE_THETA ** (-jnp.arange(n_freq, dtype=F32) / n_freq)
    ar = freqs[:, None] * jnp.arange(rows, dtype=F32)[None, :]
    ac = freqs[:, None] * jnp.arange(GRID_W, dtype=F32)[None, :]
    zr, zc = jnp.zeros_like(ar), jnp.zeros_like(ac)

    def row_table(t):
        hi = t.astype(BF16)
        lo = (t - hi.astype(F32)).astype(BF16)
        per_tile = lambda a: a.reshape(HEAD_DIM, rows // tile_rows, tile_rows).swapaxes(0, 1)
        return jnp.concatenate([per_tile(hi), per_tile(lo)], axis=-1)

    crow = row_table(jnp.concatenate([jnp.cos(ar), zr, jnp.cos(ar), zr], axis=0))
    srow = row_table(jnp.concatenate([-jnp.sin(ar), zr, jnp.sin(ar), zr], axis=0))
    ccol = jnp.tile(jnp.concatenate([zc, jnp.cos(ac), zc, jnp.cos(ac)], axis=0), (1, tile_rows))
    scol = jnp.tile(jnp.concatenate([zc, -jnp.sin(ac), zc, jnp.sin(ac)], axis=0), (1, tile_rows))
    tok_row = jnp.arange(tm, dtype=jnp.int32)[None, :] // GRID_W
    line = jnp.arange(2 * tile_rows, dtype=jnp.int32)[:, None] % tile_rows
    expand = (tok_row == line).astype(BF16)
    return crow, srow, ccol, scol, expand


def kernel(x, p, pre_mix_norm, w_in, q_norm, k_norm, rel_pos_bias, w_o, post_mix_norm,
           pre_mlp_norm, w_up, w_down, post_mlp_norm, pre_ple_norm, w_ple_gate, w_ple_proj,
           post_ple_norm):
    b, s_len, d = x.shape
    depth = w_in.shape[0]
    tm_in = 1024
    rope = _rope_tables(s_len, tm_in)
    outs = []
    for bi in range(b):
        h = x[bi]
        for i in range(depth):
            lanes = (HEAD_DIM, HEAD_DIM)
            qn = jnp.broadcast_to(_pair_major(q_norm[i] * (SM_SCALE * LOG2E))[:, None], lanes)
            kn = jnp.broadcast_to(_pair_major(k_norm[i])[:, None], lanes)
            n_rot = VA_COL * HEAD_DIM
            w_rot = _pair_major(w_in[i][:, :n_rot].reshape(d, VA_COL, HEAD_DIM)).reshape(d, n_rot)
            wat = jnp.concatenate([w_rot, w_in[i][:, n_rot:A_COLS]], axis=1).T.astype(BF16)
            proj_t, xn = _inproj_a(h, pre_mix_norm[i].reshape(1, d), wat, rope, qn, kn, tm=tm_in)
            proj = _inproj_b(xn, w_in[i][:, A_COLS:].astype(BF16))
            out_a, (wo16, wup16, wdn16, wg16) = _attn_a(
                proj_t, (w_o[i], w_up[i], w_down[i], w_ple_gate[i]))
            out_b = _attn_b(proj, rel_pos_bias[i].reshape(-1))
            h = _oproj(out_a, out_b, wo16, h, post_mix_norm[i].reshape(1, d))
            h = _mlp(h, pre_mlp_norm[i].reshape(1, d), wup16, wdn16, post_mlp_norm[i].reshape(1, d))
            h = _ple(h, p[i, bi], pre_ple_norm[i].reshape(1, d), wg16,
                     w_ple_proj[i].astype(BF16), post_ple_norm[i].reshape(1, d))
        outs.append(h)
    return jnp.stack(outs, axis=0)
```

```python
import functools
import math

import jax
import jax.numpy as jnp
from jax import lax
from jax.experimental import pallas as pl
from jax.experimental.pallas import tpu as pltpu

F32 = jnp.float32
BF16 = jnp.bfloat16

HEAD_DIM = 128
GRID_W = 64
A_Q_HEADS = 8
A_KV_HEADS = 2
A_GROUP = A_Q_HEADS // A_KV_HEADS
B_HEADS = 8
WIN_ROWS = 8
WIN_COLS = 16
ROPE_THETA = 10000.0
NORM_EPS = 1e-6
LOG2E = math.log2(math.e)
SM_SCALE = 1.0 / math.sqrt(HEAD_DIM)
NEG_BIG = -1e30

QA_COL, KA_COL, VA_COL = 0, 8, 10
A_COLS = 12 * HEAD_DIM
QB_COL, KB_COL, VB_COL = 0, 8, 16

V7X_VMEM_BYTES = 64 * 1024 * 1024
VMEM_LIMIT = V7X_VMEM_BYTES - 8 * 1024 * 1024

NB_QROWS = 4
NB_KROWS = NB_QROWS + WIN_ROWS
NB_TQ = NB_QROWS * GRID_W
NB_TK = NB_KROWS * GRID_W


def _rms(x, g):
    return x * lax.rsqrt(jnp.mean(x * x, axis=-1, keepdims=True) + NORM_EPS) * g


def _inproj_a_kernel(x_ref, g_ref, wat_ref, crow_ref, srow_ref, ccol_ref, scol_ref, expand_ref,
                     qn_ref, kn_ref, oat_ref, xn_ref):
    tm = x_ref.shape[0]
    xn = _rms(x_ref[...], g_ref[...]).astype(BF16)
    xn_ref[...] = xn
    acc = lax.dot_general(wat_ref[...], xn, (((1,), (1,)), ((), ())),
                          preferred_element_type=F32)

    def per_token(row_ref, col_ref):
        return jnp.dot(row_ref[...], expand_ref[...], preferred_element_type=F32) + col_ref[...]

    cos = per_token(crow_ref, ccol_ref)
    sin = per_token(srow_ref, scol_ref)
    half = HEAD_DIM // 2
    reps = tm // HEAD_DIM
    for hh in range(A_COLS // HEAD_DIM):
        blk = acc[hh * HEAD_DIM:(hh + 1) * HEAD_DIM, :]
        if hh < VA_COL:
            gain = jnp.tile(qn_ref[...] if hh < KA_COL else kn_ref[...], (1, reps))
            ssq = jnp.sum(blk * blk, axis=0, keepdims=True)
            y = blk * lax.rsqrt(ssq * (1.0 / HEAD_DIM) + NORM_EPS) * gain
            partner = jnp.concatenate([y[half:, :], y[:half, :]], axis=0)
            blk = y * cos + partner * sin
        oat_ref[hh * HEAD_DIM:(hh + 1) * HEAD_DIM, :] = blk.astype(BF16)


def _inproj_a(x, g, wat, rope, qn, kn, *, tm=1024):
    s_len, d = x.shape
    assert s_len % tm == 0 and tm % GRID_W == 0 and wat.shape == (A_COLS, d)
    crow, srow, ccol, scol, expand = rope
    tile_rows = tm // GRID_W
    const = lambda i: (0, 0)
    return pl.pallas_call(
        _inproj_a_kernel,
        grid=(s_len // tm,),
        in_specs=[
            pl.BlockSpec((tm, d), lambda i: (i, 0)),
            pl.BlockSpec((1, d), const),
            pl.BlockSpec(wat.shape, const, pipeline_mode=pl.Buffered(1)),
            pl.BlockSpec((None, HEAD_DIM, 2 * tile_rows), lambda i: (i, 0, 0)),
            pl.BlockSpec((None, HEAD_DIM, 2 * tile_rows), lambda i: (i, 0, 0)),
            pl.BlockSpec((HEAD_DIM, tm), const),
            pl.BlockSpec((HEAD_DIM, tm), const),
            pl.BlockSpec((2 * tile_rows, tm), const),
            pl.BlockSpec((HEAD_DIM, HEAD_DIM), const),
            pl.BlockSpec((HEAD_DIM, HEAD_DIM), const),
        ],
        out_specs=[pl.BlockSpec((A_COLS, tm), lambda i: (0, i)),
                   pl.BlockSpec((tm, d), lambda i: (i, 0))],
        out_shape=[jax.ShapeDtypeStruct((A_COLS, s_len), BF16),
                   jax.ShapeDtypeStruct((s_len, d), BF16)],
        compiler_params=pltpu.CompilerParams(
            dimension_semantics=("parallel",), vmem_limit_bytes=VMEM_LIMIT),
        name="inproj_a",
    )(x, g, wat, crow, srow, ccol, scol, expand, qn, kn)


def _inproj_b_kernel(xn_ref, w_ref, o_ref):
    j = pl.program_id(1)

    @pl.when(j == 0)
    def _():
        nq = B_HEADS * HEAD_DIM
        acc = jnp.dot(xn_ref[...], w_ref[...], preferred_element_type=F32)
        o_ref[:, :nq] = (acc[:, :nq] * (SM_SCALE * LOG2E)).astype(BF16)
        o_ref[:, nq:] = acc[:, nq:].astype(BF16)

    @pl.when(j > 0)
    def _():
        o_ref[...] = jnp.dot(xn_ref[...], w_ref[...], preferred_element_type=F32).astype(BF16)


def _inproj_b(xn, w, *, tm=1024, tn=A_COLS):
    s_len, d = xn.shape
    n = w.shape[1] - A_COLS
    assert s_len % tm == 0 and n % tn == 0 and A_COLS % tn == 0 and tn >= B_HEADS * HEAD_DIM
    skip = A_COLS // tn
    return pl.pallas_call(
        _inproj_b_kernel,
        grid=(s_len // tm, n // tn),
        in_specs=[
            pl.BlockSpec((tm, d), lambda i, j: (i, 0)),
            pl.BlockSpec((d, tn), lambda i, j: (0, j + skip)),
        ],
        out_specs=pl.BlockSpec((tm, tn), lambda i, j: (i, j)),
        out_shape=jax.ShapeDtypeStruct((s_len, n), BF16),
        compiler_params=pltpu.CompilerParams(
            dimension_semantics=("parallel", "arbitrary"), vmem_limit_bytes=VMEM_LIMIT),
        name="inproj_b",
    )(xn, w)


EXP2_SAFE_BOUND = 60.0


ONES_ROWS = 16


def _attn_a_kernel(*refs, tq, tk, nk, n_cast):
    q_ref, kt_ref, vt_ref = refs[:3]
    w32_refs = refs[3:3 + n_cast]
    o_ref = refs[3 + n_cast]
    w16_refs = refs[4 + n_cast:4 + 2 * n_cast]
    qt_ref, kn_ref, vxt_ref, kmax_ref, m_ref, acc_ref = refs[4 + 2 * n_cast:]
    i = pl.program_id(1)
    for w32, w16 in zip(w32_refs, w16_refs):
        w16[...] = w32[...].astype(BF16)

    @pl.when(i == 0)
    def _prepare_kv():
        row = lax.broadcasted_iota(jnp.int32, (ONES_ROWS, tk), 0)
        ones_rows = jnp.where(row == 0, 1.0, 0.0).astype(BF16)
        kmax = jnp.zeros((1, tk), F32)
        for c in range(nk):
            cols = slice(c * tk, (c + 1) * tk)
            ktf = kt_ref[:, cols].astype(F32)
            kn_ref[cols, :] = ktf.T.astype(BF16)
            vxt_ref[c, :HEAD_DIM, :] = vt_ref[:, cols]
            vxt_ref[c, HEAD_DIM:, :] = ones_rows
            kmax = jnp.maximum(kmax, jnp.sum(ktf * ktf, axis=0, keepdims=True))
        kmax_ref[0] = jnp.max(kmax)

    for h in range(A_GROUP):
        qt_ref[:, h * tq:(h + 1) * tq] = q_ref[h * HEAD_DIM:(h + 1) * HEAD_DIM, :]
    acc_ref[...] = jnp.zeros(acc_ref.shape, F32)
    qf = qt_ref[...].astype(F32)
    qmax = jnp.max(jnp.sum(qf * qf, axis=0, keepdims=True))
    no_max_needed = qmax * kmax_ref[0] <= EXP2_SAFE_BOUND * EXP2_SAFE_BOUND

    @pl.when(no_max_needed)
    def _plain():
        def body(j, carry):
            off = pl.multiple_of(j * tk, tk)
            s = jnp.dot(kn_ref[pl.ds(off, tk), :], qt_ref[...], preferred_element_type=F32)
            p = jnp.exp2(s).astype(BF16)
            acc_ref[...] += jnp.dot(vxt_ref[j], p, preferred_element_type=F32)
            return carry

        lax.fori_loop(0, nk, body, 0, unroll=16)

    @pl.when(jnp.logical_not(no_max_needed))
    def _online():
        m_ref[...] = jnp.full(m_ref.shape, -jnp.inf, F32)

        def body(j, carry):
            off = pl.multiple_of(j * tk, tk)
            s = jnp.dot(kn_ref[pl.ds(off, tk), :], qt_ref[...], preferred_element_type=F32)
            m_prev = m_ref[...]
            m_new = jnp.maximum(m_prev, jnp.max(s, axis=0, keepdims=True))
            p = jnp.exp2(s - m_new).astype(BF16)
            acc_ref[...] = jnp.exp2(m_prev - m_new) * acc_ref[...] + jnp.dot(
                vxt_ref[j], p, preferred_element_type=F32)
            m_ref[...] = m_new
            return carry

        lax.fori_loop(0, nk, body, 0)

    out_t = acc_ref[:HEAD_DIM, :] / acc_ref[HEAD_DIM:HEAD_DIM + 1, :]
    for h in range(A_GROUP):
        o_ref[:, h * HEAD_DIM:(h + 1) * HEAD_DIM] = out_t[:, h * tq:(h + 1) * tq].T.astype(BF16)


def _attn_a(proj_t, cast_ws, *, tq=512, tk=512):
    s_len = proj_t.shape[1]
    assert s_len % tq == 0 and s_len % tk == 0
    m_cols = A_GROUP * tq
    gw = A_GROUP * HEAD_DIM
    nk = s_len // tk
    n_i = s_len // tq
    n_steps = A_KV_HEADS * n_i
    bf16_rows = 16
    slabs = []
    for w in cast_ws:
        assert w.shape[0] % (n_steps * bf16_rows) == 0
        slabs.append(pl.BlockSpec((w.shape[0] // n_steps, w.shape[1]),
                                  lambda g, i: (g * n_i + i, 0)))
    kern = functools.partial(_attn_a_kernel, tq=tq, tk=tk, nk=nk, n_cast=len(cast_ws))
    outs = pl.pallas_call(
        kern,
        grid=(A_KV_HEADS, n_i),
        in_specs=[
            pl.BlockSpec((gw, tq), lambda g, i: (g, i)),
            pl.BlockSpec((HEAD_DIM, s_len), lambda g, i: (KA_COL + g, 0)),
            pl.BlockSpec((HEAD_DIM, s_len), lambda g, i: (VA_COL + g, 0)),
        ] + slabs,
        out_specs=[pl.BlockSpec((tq, gw), lambda g, i: (i, g))] + slabs,
        out_shape=[jax.ShapeDtypeStruct((s_len, A_Q_HEADS * HEAD_DIM), BF16)]
        + [jax.ShapeDtypeStruct(w.shape, BF16) for w in cast_ws],
        scratch_shapes=[
            pltpu.VMEM((HEAD_DIM, m_cols), BF16),
            pltpu.VMEM((s_len, HEAD_DIM), BF16),
            pltpu.VMEM((nk, HEAD_DIM + ONES_ROWS, tk), BF16),
            pltpu.SMEM((1,), F32),
            pltpu.VMEM((1, m_cols), F32),
            pltpu.VMEM((HEAD_DIM + ONES_ROWS, m_cols), F32),
        ],
        compiler_params=pltpu.CompilerParams(
            dimension_semantics=("arbitrary", "arbitrary"), vmem_limit_bytes=VMEM_LIMIT),
        name="attn_global",
    )(proj_t, proj_t, proj_t, *cast_ws)
    return outs[0], tuple(outs[1:])


def _nb_block_types(rows):
    last_q = rows - NB_QROWS
    return ((0, 0), (2 * NB_QROWS, 2 * NB_QROWS - WIN_ROWS // 2), (last_q, rows - NB_KROWS))


def _attn_b_kernel(rpb_ref, q_ref, k_ref, v_ref, o_ref, tcol_ref, tab_ref, vx_ref, *, rows, ub):
    h = pl.program_id(0)
    step = pl.program_id(1)
    nblk = rows // NB_QROWS
    n_dr = 2 * WIN_ROWS - 1
    n_dc = 2 * WIN_COLS - 1

    @pl.when(step == 0)
    def _build_tables():
        chunk = NB_TQ
        lane_v = lax.broadcasted_iota(jnp.int32, (chunk, HEAD_DIM), 1)
        ones_col = jnp.where(lane_v == 0, 1.0, 0.0).astype(BF16)

        def copy_v(c, carry):
            off_c = pl.multiple_of(c * chunk, chunk)
            vx_ref[pl.ds(off_c, chunk), :HEAD_DIM] = v_ref[pl.ds(off_c, chunk), :]
            vx_ref[pl.ds(off_c, chunk), HEAD_DIM:] = ones_col
            return carry

        lax.fori_loop(0, v_ref.shape[0] // chunk, copy_v, 0)

        c = lax.broadcasted_iota(jnp.int32, (GRID_W, 2 * GRID_W), 0)
        kc = lax.broadcasted_iota(jnp.int32, (GRID_W, 2 * GRID_W), 1) % GRID_W
        c0 = jnp.clip(c - WIN_COLS // 2, 0, GRID_W - WIN_COLS)
        col_ok = (kc >= c0) & (kc < c0 + WIN_COLS)
        dc = kc - c + (WIN_COLS - 1)
        base = h * (n_dr * n_dc)

        def row_body(a, carry):
            t = jnp.full((GRID_W, 2 * GRID_W), NEG_BIG, F32)
            for b in range(n_dc):
                t = jnp.where(dc == b, rpb_ref[base + a * n_dc + b] * LOG2E, t)
            tcol_ref[a] = jnp.where(col_ok, t, NEG_BIG)
            return carry

        lax.fori_loop(0, n_dr, row_body, 0)

        lane = lax.broadcasted_iota(jnp.int32, (GRID_W, 2 * GRID_W), 1)
        neg = jnp.full((GRID_W, 2 * GRID_W), NEG_BIG, F32)
        for t, (r_first, k_first) in enumerate(_nb_block_types(rows)):
            for qr in range(NB_QROWS):
                r = r_first + qr
                r0 = min(max(r - WIN_ROWS // 2, 0), rows - WIN_ROWS)
                for jj in range(NB_KROWS // 2):
                    halves = []
                    for kr in (2 * jj, 2 * jj + 1):
                        k_abs = k_first + kr
                        ok = r0 <= k_abs < r0 + WIN_ROWS
                        halves.append(tcol_ref[k_abs - r + WIN_ROWS - 1] if ok else neg)
                    tab_ref[t, qr * GRID_W:(qr + 1) * GRID_W,
                            jj * 2 * GRID_W:(jj + 1) * 2 * GRID_W] = jnp.where(
                                lane < GRID_W, halves[0], halves[1])

    for u in range(ub):
        pb = step * ub + u
        btype = jnp.where(pb == 0, 0, jnp.where(pb == nblk - 1, 2, 1))
        k_first = jnp.clip(pb * NB_QROWS - WIN_ROWS // 2, 0, rows - NB_KROWS)
        off = pl.multiple_of(k_first * GRID_W, GRID_W)
        ks = k_ref[pl.ds(off, NB_TK), :]
        s = lax.dot_general(q_ref[u * NB_TQ:(u + 1) * NB_TQ, :], ks, (((1,), (1,)), ((), ())),
                            preferred_element_type=F32)
        s = s + tab_ref[btype]
        m = jnp.max(s, axis=1, keepdims=True)
        p = jnp.exp2(s - m).astype(BF16)
        acc = jnp.dot(p, vx_ref[pl.ds(off, NB_TK), :], preferred_element_type=F32)
        o_ref[u * NB_TQ:(u + 1) * NB_TQ, :] = (
            acc[:, :HEAD_DIM] / acc[:, HEAD_DIM:HEAD_DIM + 1]).astype(BF16)


def _attn_b(proj, rpb_flat, *, max_ub=16):
    s_len = proj.shape[0]
    rows = s_len // GRID_W
    nblk = rows // NB_QROWS
    assert rows % NB_QROWS == 0 and rows >= 3 * NB_QROWS + WIN_ROWS
    ub = max(u for u in range(1, max_ub + 1) if nblk % u == 0)
    kern = functools.partial(_attn_b_kernel, rows=rows, ub=ub)
    return pl.pallas_call(
        kern,
        grid=(B_HEADS, nblk // ub),
        in_specs=[
            pl.BlockSpec(memory_space=pltpu.SMEM),
            pl.BlockSpec((ub * NB_TQ, HEAD_DIM), lambda h, p: (p, QB_COL + h)),
            pl.BlockSpec((s_len, HEAD_DIM), lambda h, p: (0, KB_COL + h)),
            pl.BlockSpec((s_len, HEAD_DIM), lambda h, p: (0, VB_COL + h)),
        ],
        out_specs=pl.BlockSpec((ub * NB_TQ, HEAD_DIM), lambda h, p: (p, h)),
        out_shape=jax.ShapeDtypeStruct((s_len, B_HEADS * HEAD_DIM), BF16),
        scratch_shapes=[
            pltpu.VMEM((2 * WIN_ROWS - 1, GRID_W, 2 * GRID_W), F32),
            pltpu.VMEM((3, NB_TQ, NB_TK), F32),
            pltpu.VMEM((s_len, 2 * HEAD_DIM), BF16),
        ],
        compiler_params=pltpu.CompilerParams(
            dimension_semantics=("parallel", "arbitrary"), vmem_limit_bytes=VMEM_LIMIT),
        name="attn_nbr",
    )(rpb_flat, proj, proj, proj)


ROW_SUB = 256


def _row_subblocks(tm):
    return [slice(r, r + ROW_SUB) for r in range(0, tm, ROW_SUB)]


def _oproj_kernel(oa_ref, ob_ref, wo_ref, x_ref, g_ref, h_ref):
    na = oa_ref.shape[1]
    for rows in _row_subblocks(x_ref.shape[0]):
        mix = jnp.dot(oa_ref[rows, :], wo_ref[:na, :], preferred_element_type=F32)
        mix = mix + jnp.dot(ob_ref[rows, :], wo_ref[na:, :], preferred_element_type=F32)
        h_ref[rows, :] = x_ref[rows, :] + _rms(mix, g_ref[...])


def _oproj(oa, ob, wo, x, g, *, tm=512):
    s_len, d = x.shape
    assert s_len % tm == 0 and tm % ROW_SUB == 0
    return pl.pallas_call(
        _oproj_kernel,
        grid=(s_len // tm,),
        in_specs=[
            pl.BlockSpec((tm, oa.shape[1]), lambda i: (i, 0)),
            pl.BlockSpec((tm, ob.shape[1]), lambda i: (i, 0)),
            pl.BlockSpec(wo.shape, lambda i: (0, 0), pipeline_mode=pl.Buffered(1)),
            pl.BlockSpec((tm, d), lambda i: (i, 0)),
            pl.BlockSpec((1, d), lambda i: (0, 0)),
        ],
        out_specs=pl.BlockSpec((tm, d), lambda i: (i, 0)),
        out_shape=jax.ShapeDtypeStruct((s_len, d), F32),
        compiler_params=pltpu.CompilerParams(
            dimension_semantics=("parallel",), vmem_limit_bytes=VMEM_LIMIT),
        name="oproj",
    )(oa, ob, wo, x, g)


def _mlp_kernel(h_ref, gpre_ref, wup_ref, wdn_ref, gpost_ref, o_ref, xn_ref, acc_ref):
    f = pl.program_id(1)

    def ff_chunk():
        u = jnp.dot(xn_ref[...], wup_ref[...], preferred_element_type=F32)
        a = jnp.square(jnp.maximum(u, 0.0)).astype(BF16)
        return jnp.dot(a, wdn_ref[...], preferred_element_type=F32)

    @pl.when(f == 0)
    def _():
        xn_ref[...] = _rms(h_ref[...], gpre_ref[...]).astype(BF16)
        acc_ref[...] = ff_chunk()

    @pl.when(f > 0)
    def _():
        acc_ref[...] += ff_chunk()

    @pl.when(f == pl.num_programs(1) - 1)
    def _():
        o_ref[...] = h_ref[...] + _rms(acc_ref[...], gpost_ref[...])


def _mlp(h, gpre, wup, wdn, gpost, *, tm=512, tf=1024):
    s_len, d = h.shape
    d_ff = wup.shape[1]
    assert s_len % tm == 0 and d_ff % tf == 0
    return pl.pallas_call(
        _mlp_kernel,
        grid=(s_len // tm, d_ff // tf),
        in_specs=[
            pl.BlockSpec((tm, d), lambda i, f: (i, 0)),
            pl.BlockSpec((1, d), lambda i, f: (0, 0)),
            pl.BlockSpec((d, tf), lambda i, f: (0, f)),
            pl.BlockSpec((tf, d), lambda i, f: (f, 0)),
            pl.BlockSpec((1, d), lambda i, f: (0, 0)),
        ],
        out_specs=pl.BlockSpec((tm, d), lambda i, f: (i, 0)),
        out_shape=jax.ShapeDtypeStruct((s_len, d), F32),
        scratch_shapes=[pltpu.VMEM((tm, d), BF16), pltpu.VMEM((tm, d), F32)],
        compiler_params=pltpu.CompilerParams(
            dimension_semantics=("parallel", "arbitrary"), vmem_limit_bytes=VMEM_LIMIT),
        name="mlp",
    )(h, gpre, wup, wdn, gpost)


def _ple_kernel(h_ref, p_ref, gpre_ref, wg_ref, wp_ref, gpost_ref, o_ref):
    for rows in _row_subblocks(h_ref.shape[0]):
        h = h_ref[rows, :]
        xn = _rms(h, gpre_ref[...]).astype(BF16)
        gate = jax.nn.sigmoid(jnp.dot(xn, wg_ref[...], preferred_element_type=F32))
        e = jnp.dot(p_ref[rows, :].astype(BF16), wp_ref[...], preferred_element_type=F32) * gate
        o_ref[rows, :] = h + _rms(e, gpost_ref[...])


def _ple(h, p, gpre, wg, wp, gpost, *, tm=512):
    s_len, d = h.shape
    dp = p.shape[1]
    assert s_len % tm == 0 and tm % ROW_SUB == 0
    return pl.pallas_call(
        _ple_kernel,
        grid=(s_len // tm,),
        in_specs=[
            pl.BlockSpec((tm, d), lambda i: (i, 0)),
            pl.BlockSpec((tm, dp), lambda i: (i, 0)),
            pl.BlockSpec((1, d), lambda i: (0, 0)),
            pl.BlockSpec(wg.shape, lambda i: (0, 0), pipeline_mode=pl.Buffered(1)),
            pl.BlockSpec(wp.shape, lambda i: (0, 0), pipeline_mode=pl.Buffered(1)),
            pl.BlockSpec((1, d), lambda i: (0, 0)),
        ],
        out_specs=pl.BlockSpec((tm, d), lambda i: (i, 0)),
        out_shape=jax.ShapeDtypeStruct((s_len, d), F32),
        compiler_params=pltpu.CompilerParams(
            dimension_semantics=("parallel",), vmem_limit_bytes=VMEM_LIMIT),
        name="ple",
    )(h, p, gpre, wg, wp, gpost)


def _pair_major(a):
    lead = a.shape[:-1]
    return a.reshape(lead + (2, 2, HEAD_DIM // 4)).swapaxes(-3, -2).reshape(lead + (HEAD_DIM,))


def _rope_tables(s_len, tm):
    rows = s_len // GRID_W
    tile_rows = tm // GRID_W
    n_freq = HEAD_DIM // 4
    freqs = ROPE_THETA ** (-jnp.arange(n_freq, dtype=F32) / n_freq)
    ar = freqs[:, None] * jnp.arange(rows, dtype=F32)[None, :]
    ac = freqs[:, None] * jnp.arange(GRID_W, dtype=F32)[None, :]
    zr, zc = jnp.zeros_like(ar), jnp.zeros_like(ac)

    def row_table(t):
        hi = t.astype(BF16)
        lo = (t - hi.astype(F32)).astype(BF16)
        per_tile = lambda a: a.reshape(HEAD_DIM, rows // tile_rows, tile_rows).swapaxes(0, 1)
        return jnp.concatenate([per_tile(hi), per_tile(lo)], axis=-1)

    crow = row_table(jnp.concatenate([jnp.cos(ar), zr, jnp.cos(ar), zr], axis=0))
    srow = row_table(jnp.concatenate([-jnp.sin(ar), zr, jnp.sin(ar), zr], axis=0))
    ccol = jnp.tile(jnp.concatenate([zc, jnp.cos(ac), zc, jnp.cos(ac)], axis=0), (1, tile_rows))
    scol = jnp.tile(jnp.concatenate([zc, -jnp.sin(ac), zc, jnp.sin(ac)], axis=0), (1, tile_rows))
    tok_row = jnp.arange(tm, dtype=jnp.int32)[None, :] // GRID_W
    line = jnp.arange(2 * tile_rows, dtype=jnp.int32)[:, None] % tile_rows
    expand = (tok_row == line).astype(BF16)
    return crow, srow, ccol, scol, expand


def kernel(x, p, pre_mix_norm, w_in, q_norm, k_norm, rel_pos_bias, w_o, post_mix_norm,
           pre_mlp_norm, w_up, w_down, post_mlp_norm, pre_ple_norm, w_ple_gate, w_ple_proj,
           post_ple_norm):
    b, s_len, d = x.shape
    depth = w_in.shape[0]
    tm_in = 1024
    rope = _rope_tables(s_len, tm_in)
    outs = []
    for bi in range(b):
        h = x[bi]
        for i in range(depth):
            lanes = (HEAD_DIM, HEAD_DIM)
            qn = jnp.broadcast_to(_pair_major(q_norm[i] * (SM_SCALE * LOG2E))[:, None], lanes)
            kn = jnp.broadcast_to(_pair_major(k_norm[i])[:, None], lanes)
            n_rot = VA_COL * HEAD_DIM
            w16 = w_in[i].astype(BF16)
            w_rot = _pair_major(w16[:, :n_rot].reshape(d, VA_COL, HEAD_DIM)).reshape(d, n_rot)
            wat = jnp.concatenate([w_rot, w16[:, n_rot:A_COLS]], axis=1).T
            proj_t, xn = _inproj_a(h, pre_mix_norm[i].reshape(1, d), wat, rope, qn, kn, tm=tm_in)
            proj = _inproj_b(xn, w16)
            out_a, (wo16, wup16, wdn16, wg16) = _attn_a(
                proj_t, (w_o[i], w_up[i], w_down[i], w_ple_gate[i]))
            out_b = _attn_b(proj, rel_pos_bias[i].reshape(-1))
            h = _oproj(out_a, out_b, wo16, h, post_mix_norm[i].reshape(1, d))
            h = _mlp(h, pre_mlp_norm[i].reshape(1, d), wup16, wdn16, post_mlp_norm[i].reshape(1, d))
            h = _ple(h, p[i, bi], pre_ple_norm[i].reshape(1, d), wg16,
                     w_ple_proj[i].astype(BF16), post_ple_norm[i].reshape(1, d))
        outs.append(h)
    return jnp.stack(outs, axis=0)
```

```python
import functools
import math

import jax
import jax.numpy as jnp
from jax import lax
from jax.experimental import pallas as pl
from jax.experimental.pallas import tpu as pltpu

F32 = jnp.float32
BF16 = jnp.bfloat16

HEAD_DIM = 128
GRID_W = 64
A_Q_HEADS = 8
A_KV_HEADS = 2
A_GROUP = A_Q_HEADS // A_KV_HEADS
B_HEADS = 8
WIN_ROWS = 8
WIN_COLS = 16
ROPE_THETA = 10000.0
NORM_EPS = 1e-6
LOG2E = math.log2(math.e)
SM_SCALE = 1.0 / math.sqrt(HEAD_DIM)
NEG_BIG = -1e30

QA_COL, KA_COL, VA_COL = 0, 8, 10
A_COLS = 12 * HEAD_DIM
QB_COL, KB_COL, VB_COL = 0, 8, 16

V7X_VMEM_BYTES = 64 * 1024 * 1024
VMEM_LIMIT = V7X_VMEM_BYTES - 8 * 1024 * 1024

NB_QROWS = 4
NB_KROWS = NB_QROWS + WIN_ROWS
NB_TQ = NB_QROWS * GRID_W
NB_TK = NB_KROWS * GRID_W


def _rms(x, g):
    return x * lax.rsqrt(jnp.mean(x * x, axis=-1, keepdims=True) + NORM_EPS) * g


def _inproj_a_kernel(x_ref, g_ref, wat_ref, crow_ref, srow_ref, ccol_ref, scol_ref, expand_ref,
                     qn_ref, kn_ref, *rest):
    *wsrc_refs, oat_ref, xn_ref, wb_ref = rest
    for c, wsrc in enumerate(wsrc_refs):
        wb_ref[:, c * A_COLS:(c + 1) * A_COLS] = wsrc[...].astype(BF16)
    tm = x_ref.shape[0]
    xn = _rms(x_ref[...], g_ref[...]).astype(BF16)
    xn_ref[...] = xn
    acc = lax.dot_general(wat_ref[...], xn, (((1,), (1,)), ((), ())),
                          preferred_element_type=F32)

    def per_token(row_ref, col_ref):
        return jnp.dot(row_ref[...], expand_ref[...], preferred_element_type=F32) + col_ref[...]

    cos = per_token(crow_ref, ccol_ref)
    sin = per_token(srow_ref, scol_ref)
    half = HEAD_DIM // 2
    reps = tm // HEAD_DIM
    for hh in range(A_COLS // HEAD_DIM):
        blk = acc[hh * HEAD_DIM:(hh + 1) * HEAD_DIM, :]
        if hh < VA_COL:
            gain = jnp.tile(qn_ref[...] if hh < KA_COL else kn_ref[...], (1, reps))
            ssq = jnp.sum(blk * blk, axis=0, keepdims=True)
            y = blk * lax.rsqrt(ssq * (1.0 / HEAD_DIM) + NORM_EPS) * gain
            partner = jnp.concatenate([y[half:, :], y[:half, :]], axis=0)
            blk = y * cos + partner * sin
        oat_ref[hh * HEAD_DIM:(hh + 1) * HEAD_DIM, :] = blk.astype(BF16)


def _inproj_a(x, g, wat, w_full, rope, qn, kn, *, tm=1024):
    s_len, d = x.shape
    assert s_len % tm == 0 and tm % GRID_W == 0 and wat.shape == (A_COLS, d)
    crow, srow, ccol, scol, expand = rope
    tile_rows = tm // GRID_W
    const = lambda i: (0, 0)
    n_steps = s_len // tm
    n_b = w_full.shape[1] - A_COLS
    slab = d // n_steps
    assert n_b % A_COLS == 0 and d % (n_steps * 16) == 0
    wsrc_specs = [pl.BlockSpec((slab, A_COLS), lambda i, c=c: (i, c))
                  for c in range(1, 1 + n_b // A_COLS)]
    return pl.pallas_call(
        _inproj_a_kernel,
        grid=(s_len // tm,),
        in_specs=[
            pl.BlockSpec((tm, d), lambda i: (i, 0)),
            pl.BlockSpec((1, d), const),
            pl.BlockSpec(wat.shape, const, pipeline_mode=pl.Buffered(1)),
            pl.BlockSpec((None, HEAD_DIM, 2 * tile_rows), lambda i: (i, 0, 0)),
            pl.BlockSpec((None, HEAD_DIM, 2 * tile_rows), lambda i: (i, 0, 0)),
            pl.BlockSpec((HEAD_DIM, tm), const),
            pl.BlockSpec((HEAD_DIM, tm), const),
            pl.BlockSpec((2 * tile_rows, tm), const),
            pl.BlockSpec((HEAD_DIM, HEAD_DIM), const),
            pl.BlockSpec((HEAD_DIM, HEAD_DIM), const),
        ] + wsrc_specs,
        out_specs=[pl.BlockSpec((A_COLS, tm), lambda i: (0, i)),
                   pl.BlockSpec((tm, d), lambda i: (i, 0)),
                   pl.BlockSpec((slab, n_b), lambda i: (i, 0))],
        out_shape=[jax.ShapeDtypeStruct((A_COLS, s_len), BF16),
                   jax.ShapeDtypeStruct((s_len, d), BF16),
                   jax.ShapeDtypeStruct((d, n_b), BF16)],
        compiler_params=pltpu.CompilerParams(
            dimension_semantics=("parallel",), vmem_limit_bytes=VMEM_LIMIT),
        name="inproj_a",
    )(x, g, wat, crow, srow, ccol, scol, expand, qn, kn, *([w_full] * len(wsrc_specs)))


def _inproj_b_kernel(xn_ref, w_ref, o_ref):
    j = pl.program_id(1)

    @pl.when(j == 0)
    def _():
        nq = B_HEADS * HEAD_DIM
        acc = jnp.dot(xn_ref[...], w_ref[...], preferred_element_type=F32)
        o_ref[:, :nq] = (acc[:, :nq] * (SM_SCALE * LOG2E)).astype(BF16)
        o_ref[:, nq:] = acc[:, nq:].astype(BF16)

    @pl.when(j > 0)
    def _():
        o_ref[...] = jnp.dot(xn_ref[...], w_ref[...], preferred_element_type=F32).astype(BF16)


def _inproj_b(xn, w, *, tm=1024, tn=A_COLS):
    s_len, d = xn.shape
    n = w.shape[1]
    assert s_len % tm == 0 and n % tn == 0 and tn >= B_HEADS * HEAD_DIM
    return pl.pallas_call(
        _inproj_b_kernel,
        grid=(s_len // tm, n // tn),
        in_specs=[
            pl.BlockSpec((tm, d), lambda i, j: (i, 0)),
            pl.BlockSpec((d, tn), lambda i, j: (0, j)),
        ],
        out_specs=pl.BlockSpec((tm, tn), lambda i, j: (i, j)),
        out_shape=jax.ShapeDtypeStruct((s_len, n), BF16),
        compiler_params=pltpu.CompilerParams(
            dimension_semantics=("parallel", "arbitrary"), vmem_limit_bytes=VMEM_LIMIT),
        name="inproj_b",
    )(xn, w)


EXP2_SAFE_BOUND = 60.0


ONES_ROWS = 16


def _attn_a_kernel(*refs, tq, tk, nk, n_cast):
    q_ref, kt_ref, vt_ref = refs[:3]
    w32_refs = refs[3:3 + n_cast]
    o_ref = refs[3 + n_cast]
    w16_refs = refs[4 + n_cast:4 + 2 * n_cast]
    qt_ref, kn_ref, vxt_ref, kmax_ref, m_ref, acc_ref = refs[4 + 2 * n_cast:]
    i = pl.program_id(1)
    for w32, w16 in zip(w32_refs, w16_refs):
        w16[...] = w32[...].astype(BF16)

    @pl.when(i == 0)
    def _prepare_kv():
        row = lax.broadcasted_iota(jnp.int32, (ONES_ROWS, tk), 0)
        ones_rows = jnp.where(row == 0, 1.0, 0.0).astype(BF16)
        kmax = jnp.zeros((1, tk), F32)
        for c in range(nk):
            cols = slice(c * tk, (c + 1) * tk)
            ktf = kt_ref[:, cols].astype(F32)
            kn_ref[cols, :] = ktf.T.astype(BF16)
            vxt_ref[c, :HEAD_DIM, :] = vt_ref[:, cols]
            vxt_ref[c, HEAD_DIM:, :] = ones_rows
            kmax = jnp.maximum(kmax, jnp.sum(ktf * ktf, axis=0, keepdims=True))
        kmax_ref[0] = jnp.max(kmax)

    for h in range(A_GROUP):
        qt_ref[:, h * tq:(h + 1) * tq] = q_ref[h * HEAD_DIM:(h + 1) * HEAD_DIM, :]
    acc_ref[...] = jnp.zeros(acc_ref.shape, F32)
    qf = qt_ref[...].astype(F32)
    qmax = jnp.max(jnp.sum(qf * qf, axis=0, keepdims=True))
    no_max_needed = qmax * kmax_ref[0] <= EXP2_SAFE_BOUND * EXP2_SAFE_BOUND

    @pl.when(no_max_needed)
    def _plain():
        def body(j, carry):
            off = pl.multiple_of(j * tk, tk)
            s = jnp.dot(kn_ref[pl.ds(off, tk), :], qt_ref[...], preferred_element_type=F32)
            p = jnp.exp2(s).astype(BF16)
            acc_ref[...] += jnp.dot(vxt_ref[j], p, preferred_element_type=F32)
            return carry

        lax.fori_loop(0, nk, body, 0, unroll=16)

    @pl.when(jnp.logical_not(no_max_needed))
    def _online():
        m_ref[...] = jnp.full(m_ref.shape, -jnp.inf, F32)

        def body(j, carry):
            off = pl.multiple_of(j * tk, tk)
            s = jnp.dot(kn_ref[pl.ds(off, tk), :], qt_ref[...], preferred_element_type=F32)
            m_prev = m_ref[...]
            m_new = jnp.maximum(m_prev, jnp.max(s, axis=0, keepdims=True))
            p = jnp.exp2(s - m_new).astype(BF16)
            acc_ref[...] = jnp.exp2(m_prev - m_new) * acc_ref[...] + jnp.dot(
                vxt_ref[j], p, preferred_element_type=F32)
            m_ref[...] = m_new
            return carry

        lax.fori_loop(0, nk, body, 0)

    out_t = acc_ref[:HEAD_DIM, :] / acc_ref[HEAD_DIM:HEAD_DIM + 1, :]
    for h in range(A_GROUP):
        o_ref[:, h * HEAD_DIM:(h + 1) * HEAD_DIM] = out_t[:, h * tq:(h + 1) * tq].T.astype(BF16)


def _attn_a(proj_t, cast_ws, *, tq=512, tk=512):
    s_len = proj_t.shape[1]
    assert s_len % tq == 0 and s_len % tk == 0
    m_cols = A_GROUP * tq
    gw = A_GROUP * HEAD_DIM
    nk = s_len // tk
    n_i = s_len // tq
    n_steps = A_KV_HEADS * n_i
    bf16_rows = 16
    slabs = []
    for w in cast_ws:
        assert w.shape[0] % (n_steps * bf16_rows) == 0
        slabs.append(pl.BlockSpec((w.shape[0] // n_steps, w.shape[1]),
                                  lambda g, i: (g * n_i + i, 0)))
    kern = functools.partial(_attn_a_kernel, tq=tq, tk=tk, nk=nk, n_cast=len(cast_ws))
    outs = pl.pallas_call(
        kern,
        grid=(A_KV_HEADS, n_i),
        in_specs=[
            pl.BlockSpec((gw, tq), lambda g, i: (g, i)),
            pl.BlockSpec((HEAD_DIM, s_len), lambda g, i: (KA_COL + g, 0)),
            pl.BlockSpec((HEAD_DIM, s_len), lambda g, i: (VA_COL + g, 0)),
        ] + slabs,
        out_specs=[pl.BlockSpec((tq, gw), lambda g, i: (i, g))] + slabs,
        out_shape=[jax.ShapeDtypeStruct((s_len, A_Q_HEADS * HEAD_DIM), BF16)]
        + [jax.ShapeDtypeStruct(w.shape, BF16) for w in cast_ws],
        scratch_shapes=[
            pltpu.VMEM((HEAD_DIM, m_cols), BF16),
            pltpu.VMEM((s_len, HEAD_DIM), BF16),
            pltpu.VMEM((nk, HEAD_DIM + ONES_ROWS, tk), BF16),
            pltpu.SMEM((1,), F32),
            pltpu.VMEM((1, m_cols), F32),
            pltpu.VMEM((HEAD_DIM + ONES_ROWS, m_cols), F32),
        ],
        compiler_params=pltpu.CompilerParams(
            dimension_semantics=("arbitrary", "arbitrary"), vmem_limit_bytes=VMEM_LIMIT),
        name="attn_global",
    )(proj_t, proj_t, proj_t, *cast_ws)
    return outs[0], tuple(outs[1:])


def _nb_block_types(rows):
    last_q = rows - NB_QROWS
    return ((0, 0), (2 * NB_QROWS, 2 * NB_QROWS - WIN_ROWS // 2), (last_q, rows - NB_KROWS))


def _attn_b_kernel(rpb_ref, q_ref, k_ref, v_ref, o_ref, tcol_ref, tab_ref, vx_ref, *, rows, ub):
    h = pl.program_id(0)
    step = pl.program_id(1)
    nblk = rows // NB_QROWS
    n_dr = 2 * WIN_ROWS - 1
    n_dc = 2 * WIN_COLS - 1

    @pl.when(step == 0)
    def _build_tables():
        chunk = NB_TQ
        lane_v = lax.broadcasted_iota(jnp.int32, (chunk, HEAD_DIM), 1)
        ones_col = jnp.where(lane_v == 0, 1.0, 0.0).astype(BF16)

        def copy_v(c, carry):
            off_c = pl.multiple_of(c * chunk, chunk)
            vx_ref[pl.ds(off_c, chunk), :HEAD_DIM] = v_ref[pl.ds(off_c, chunk), :]
            vx_ref[pl.ds(off_c, chunk), HEAD_DIM:] = ones_col
            return carry

        lax.fori_loop(0, v_ref.shape[0] // chunk, copy_v, 0)

        c = lax.broadcasted_iota(jnp.int32, (GRID_W, 2 * GRID_W), 0)
        kc = lax.broadcasted_iota(jnp.int32, (GRID_W, 2 * GRID_W), 1) % GRID_W
        c0 = jnp.clip(c - WIN_COLS // 2, 0, GRID_W - WIN_COLS)
        col_ok = (kc >= c0) & (kc < c0 + WIN_COLS)
        dc = kc - c + (WIN_COLS - 1)
        base = h * (n_dr * n_dc)

        def row_body(a, carry):
            t = jnp.full((GRID_W, 2 * GRID_W), NEG_BIG, F32)
            for b in range(n_dc):
                t = jnp.where(dc == b, rpb_ref[base + a * n_dc + b] * LOG2E, t)
            tcol_ref[a] = jnp.where(col_ok, t, NEG_BIG)
            return carry

        lax.fori_loop(0, n_dr, row_body, 0)

        lane = lax.broadcasted_iota(jnp.int32, (GRID_W, 2 * GRID_W), 1)
        neg = jnp.full((GRID_W, 2 * GRID_W), NEG_BIG, F32)
        for t, (r_first, k_first) in enumerate(_nb_block_types(rows)):
            for qr in range(NB_QROWS):
                r = r_first + qr
                r0 = min(max(r - WIN_ROWS // 2, 0), rows - WIN_ROWS)
                for jj in range(NB_KROWS // 2):
                    halves = []
                    for kr in (2 * jj, 2 * jj + 1):
                        k_abs = k_first + kr
                        ok = r0 <= k_abs < r0 + WIN_ROWS
                        halves.append(tcol_ref[k_abs - r + WIN_ROWS - 1] if ok else neg)
                    tab_ref[t, qr * GRID_W:(qr + 1) * GRID_W,
                            jj * 2 * GRID_W:(jj + 1) * 2 * GRID_W] = jnp.where(
                                lane < GRID_W, halves[0], halves[1])

    for u in range(ub):
        pb = step * ub + u
        btype = jnp.where(pb == 0, 0, jnp.where(pb == nblk - 1, 2, 1))
        k_first = jnp.clip(pb * NB_QROWS - WIN_ROWS // 2, 0, rows - NB_KROWS)
        off = pl.multiple_of(k_first * GRID_W, GRID_W)
        ks = k_ref[pl.ds(off, NB_TK), :]
        s = lax.dot_general(q_ref[u * NB_TQ:(u + 1) * NB_TQ, :], ks, (((1,), (1,)), ((), ())),
                            preferred_element_type=F32)
        s = s + tab_ref[btype]
        m = jnp.max(s, axis=1, keepdims=True)
        p = jnp.exp2(s - m).astype(BF16)
        acc = jnp.dot(p, vx_ref[pl.ds(off, NB_TK), :], preferred_element_type=F32)
        o_ref[u * NB_TQ:(u + 1) * NB_TQ, :] = (
            acc[:, :HEAD_DIM] / acc[:, HEAD_DIM:HEAD_DIM + 1]).astype(BF16)


def _attn_b(proj, rpb_flat, *, max_ub=16):
    s_len = proj.shape[0]
    rows = s_len // GRID_W
    nblk = rows // NB_QROWS
    assert rows % NB_QROWS == 0 and rows >= 3 * NB_QROWS + WIN_ROWS
    ub = max(u for u in range(1, max_ub + 1) if nblk % u == 0)
    kern = functools.partial(_attn_b_kernel, rows=rows, ub=ub)
    return pl.pallas_call(
        kern,
        grid=(B_HEADS, nblk // ub),
        in_specs=[
            pl.BlockSpec(memory_space=pltpu.SMEM),
            pl.BlockSpec((ub * NB_TQ, HEAD_DIM), lambda h, p: (p, QB_COL + h)),
            pl.BlockSpec((s_len, HEAD_DIM), lambda h, p: (0, KB_COL + h)),
            pl.BlockSpec((s_len, HEAD_DIM), lambda h, p: (0, VB_COL + h)),
        ],
        out_specs=pl.BlockSpec((ub * NB_TQ, HEAD_DIM), lambda h, p: (p, h)),
        out_shape=jax.ShapeDtypeStruct((s_len, B_HEADS * HEAD_DIM), BF16),
        scratch_shapes=[
            pltpu.VMEM((2 * WIN_ROWS - 1, GRID_W, 2 * GRID_W), F32),
            pltpu.VMEM((3, NB_TQ, NB_TK), F32),
            pltpu.VMEM((s_len, 2 * HEAD_DIM), BF16),
        ],
        compiler_params=pltpu.CompilerParams(
            dimension_semantics=("parallel", "arbitrary"), vmem_limit_bytes=VMEM_LIMIT),
        name="attn_nbr",
    )(rpb_flat, proj, proj, proj)


ROW_SUB = 256


def _row_subblocks(tm):
    return [slice(r, r + ROW_SUB) for r in range(0, tm, ROW_SUB)]


def _oproj_kernel(oa_ref, ob_ref, wo_ref, x_ref, g_ref, h_ref):
    na = oa_ref.shape[1]
    for rows in _row_subblocks(x_ref.shape[0]):
        mix = jnp.dot(oa_ref[rows, :], wo_ref[:na, :], preferred_element_type=F32)
        mix = mix + jnp.dot(ob_ref[rows, :], wo_ref[na:, :], preferred_element_type=F32)
        h_ref[rows, :] = x_ref[rows, :] + _rms(mix, g_ref[...])


def _oproj(oa, ob, wo, x, g, *, tm=512):
    s_len, d = x.shape
    assert s_len % tm == 0 and tm % ROW_SUB == 0
    return pl.pallas_call(
        _oproj_kernel,
        grid=(s_len // tm,),
        in_specs=[
            pl.BlockSpec((tm, oa.shape[1]), lambda i: (i, 0)),
            pl.BlockSpec((tm, ob.shape[1]), lambda i: (i, 0)),
            pl.BlockSpec(wo.shape, lambda i: (0, 0), pipeline_mode=pl.Buffered(1)),
            pl.BlockSpec((tm, d), lambda i: (i, 0)),
            pl.BlockSpec((1, d), lambda i: (0, 0)),
        ],
        out_specs=pl.BlockSpec((tm, d), lambda i: (i, 0)),
        out_shape=jax.ShapeDtypeStruct((s_len, d), F32),
        compiler_params=pltpu.CompilerParams(
            dimension_semantics=("parallel",), vmem_limit_bytes=VMEM_LIMIT),
        name="oproj",
    )(oa, ob, wo, x, g)


def _mlp_kernel(h_ref, gpre_ref, wup_ref, wdn_ref, gpost_ref, o_ref, xn_ref, acc_ref):
    f = pl.program_id(1)

    def ff_chunk():
        u = jnp.dot(xn_ref[...], wup_ref[...], preferred_element_type=F32)
        a = jnp.square(jnp.maximum(u, 0.0)).astype(BF16)
        return jnp.dot(a, wdn_ref[...], preferred_element_type=F32)

    @pl.when(f == 0)
    def _():
        xn_ref[...] = _rms(h_ref[...], gpre_ref[...]).astype(BF16)
        acc_ref[...] = ff_chunk()

    @pl.when(f > 0)
    def _():
        acc_ref[...] += ff_chunk()

    @pl.when(f == pl.num_programs(1) - 1)
    def _():
        o_ref[...] = h_ref[...] + _rms(acc_ref[...], gpost_ref[...])


def _mlp(h, gpre, wup, wdn, gpost, *, tm=512, tf=1024):
    s_len, d = h.shape
    d_ff = wup.shape[1]
    assert s_len % tm == 0 and d_ff % tf == 0
    return pl.pallas_call(
        _mlp_kernel,
        grid=(s_len // tm, d_ff // tf),
        in_specs=[
            pl.BlockSpec((tm, d), lambda i, f: (i, 0)),
            pl.BlockSpec((1, d), lambda i, f: (0, 0)),
            pl.BlockSpec((d, tf), lambda i, f: (0, f)),
            pl.BlockSpec((tf, d), lambda i, f: (f, 0)),
            pl.BlockSpec((1, d), lambda i, f: (0, 0)),
        ],
        out_specs=pl.BlockSpec((tm, d), lambda i, f: (i, 0)),
        out_shape=jax.ShapeDtypeStruct((s_len, d), F32),
        scratch_shapes=[pltpu.VMEM((tm, d), BF16), pltpu.VMEM((tm, d), F32)],
        compiler_params=pltpu.CompilerParams(
            dimension_semantics=("parallel", "arbitrary"), vmem_limit_bytes=VMEM_LIMIT),
        name="mlp",
    )(h, gpre, wup, wdn, gpost)


def _ple_kernel(h_ref, p_ref, gpre_ref, wg_ref, wp_ref, gpost_ref, o_ref):
    for rows in _row_subblocks(h_ref.shape[0]):
        h = h_ref[rows, :]
        xn = _rms(h, gpre_ref[...]).astype(BF16)
        gate = jax.nn.sigmoid(jnp.dot(xn, wg_ref[...], preferred_element_type=F32))
        e = jnp.dot(p_ref[rows, :].astype(BF16), wp_ref[...], preferred_element_type=F32) * gate
        o_ref[rows, :] = h + _rms(e, gpost_ref[...])


def _ple(h, p, gpre, wg, wp, gpost, *, tm=512):
    s_len, d = h.shape
    dp = p.shape[1]
    assert s_len % tm == 0 and tm % ROW_SUB == 0
    return pl.pallas_call(
        _ple_kernel,
        grid=(s_len // tm,),
        in_specs=[
            pl.BlockSpec((tm, d), lambda i: (i, 0)),
            pl.BlockSpec((tm, dp), lambda i: (i, 0)),
            pl.BlockSpec((1, d), lambda i: (0, 0)),
            pl.BlockSpec(wg.shape, lambda i: (0, 0), pipeline_mode=pl.Buffered(1)),
            pl.BlockSpec(wp.shape, lambda i: (0, 0), pipeline_mode=pl.Buffered(1)),
            pl.BlockSpec((1, d), lambda i: (0, 0)),
        ],
        out_specs=pl.BlockSpec((tm, d), lambda i: (i, 0)),
        out_shape=jax.ShapeDtypeStruct((s_len, d), F32),
        compiler_params=pltpu.CompilerParams(
            dimension_semantics=("parallel",), vmem_limit_bytes=VMEM_LIMIT),
        name="ple",
    )(h, p, gpre, wg, wp, gpost)


def _pair_major(a):
    lead = a.shape[:-1]
    return a.reshape(lead + (2, 2, HEAD_DIM // 4)).swapaxes(-3, -2).reshape(lead + (HEAD_DIM,))


def _rope_tables(s_len, tm):
    rows = s_len // GRID_W
    tile_rows = tm // GRID_W
    n_freq = HEAD_DIM // 4
    freqs = ROPE_THETA ** (-jnp.arange(n_freq, dtype=F32) / n_freq)
    ar = freqs[:, None] * jnp.arange(rows, dtype=F32)[None, :]
    ac = freqs[:, None] * jnp.arange(GRID_W, dtype=F32)[None, :]
    zr, zc = jnp.zeros_like(ar), jnp.zeros_like(ac)

    def row_table(t):
        hi = t.astype(BF16)
        lo = (t - hi.astype(F32)).astype(BF16)
        per_tile = lambda a: a.reshape(HEAD_DIM, rows // tile_rows, tile_rows).swapaxes(0, 1)
        return jnp.concatenate([per_tile(hi), per_tile(lo)], axis=-1)

    crow = row_table(jnp.concatenate([jnp.cos(ar), zr, jnp.cos(ar), zr], axis=0))
    srow = row_table(jnp.concatenate([-jnp.sin(ar), zr, jnp.sin(ar), zr], axis=0))
    ccol = jnp.tile(jnp.concatenate([zc, jnp.cos(ac), zc, jnp.cos(ac)], axis=0), (1, tile_rows))
    scol = jnp.tile(jnp.concatenate([zc, -jnp.sin(ac), zc, jnp.sin(ac)], axis=0), (1, tile_rows))
    tok_row = jnp.arange(tm, dtype=jnp.int32)[None, :] // GRID_W
    line = jnp.arange(2 * tile_rows, dtype=jnp.int32)[:, None] % tile_rows
    expand = (tok_row == line).astype(BF16)
    return crow, srow, ccol, scol, expand


def kernel(x, p, pre_mix_norm, w_in, q_norm, k_norm, rel_pos_bias, w_o, post_mix_norm,
           pre_mlp_norm, w_up, w_down, post_mlp_norm, pre_ple_norm, w_ple_gate, w_ple_proj,
           post_ple_norm):
    b, s_len, d = x.shape
    depth = w_in.shape[0]
    tm_in = 1024
    rope = _rope_tables(s_len, tm_in)
    outs = []
    for bi in range(b):
        h = x[bi]
        for i in range(depth):
            lanes = (HEAD_DIM, HEAD_DIM)
            qn = jnp.broadcast_to(_pair_major(q_norm[i] * (SM_SCALE * LOG2E))[:, None], lanes)
            kn = jnp.broadcast_to(_pair_major(k_norm[i])[:, None], lanes)
            n_rot = VA_COL * HEAD_DIM
            wa16 = w_in[i][:, :A_COLS].astype(BF16)
            w_rot = _pair_major(wa16[:, :n_rot].reshape(d, VA_COL, HEAD_DIM)).reshape(d, n_rot)
            wat = jnp.concatenate([w_rot, wa16[:, n_rot:]], axis=1).T
            proj_t, xn, wb16 = _inproj_a(h, pre_mix_norm[i].reshape(1, d), wat, w_in[i], rope,
                                         qn, kn, tm=tm_in)
            proj = _inproj_b(xn, wb16)
            out_a, (wo16, wup16, wdn16, wg16) = _attn_a(
                proj_t, (w_o[i], w_up[i], w_down[i], w_ple_gate[i]))
            out_b = _attn_b(proj, rel_pos_bias[i].reshape(-1))
            h = _oproj(out_a, out_b, wo16, h, post_mix_norm[i].reshape(1, d))
            h = _mlp(h, pre_mlp_norm[i].reshape(1, d), wup16, wdn16, post_mlp_norm[i].reshape(1, d))
            h = _ple(h, p[i, bi], pre_ple_norm[i].reshape(1, d), wg16,
                     w_ple_proj[i].astype(BF16), post_ple_norm[i].reshape(1, d))
        outs.append(h)
    return jnp.stack(outs, axis=0)
```

```python
import functools
import math

import jax
import jax.numpy as jnp
from jax import lax
from jax.experimental import pallas as pl
from jax.experimental.pallas import tpu as pltpu

F32 = jnp.float32
BF16 = jnp.bfloat16

HEAD_DIM = 128
GRID_W = 64
A_Q_HEADS = 8
A_KV_HEADS = 2
A_GROUP = A_Q_HEADS // A_KV_HEADS
B_HEADS = 8
WIN_ROWS = 8
WIN_COLS = 16
ROPE_THETA = 10000.0
NORM_EPS = 1e-6
LOG2E = math.log2(math.e)
SM_SCALE = 1.0 / math.sqrt(HEAD_DIM)
NEG_BIG = -1e30

QA_COL, KA_COL, VA_COL = 0, 8, 10
A_COLS = 12 * HEAD_DIM
QB_COL, KB_COL, VB_COL = 0, 8, 16

V7X_VMEM_BYTES = 64 * 1024 * 1024
VMEM_LIMIT = V7X_VMEM_BYTES - 8 * 1024 * 1024

NB_QROWS = 4
NB_KROWS = NB_QROWS + WIN_ROWS
NB_TQ = NB_QROWS * GRID_W
NB_TK = NB_KROWS * GRID_W


def _rms(x, g):
    return x * lax.rsqrt(jnp.mean(x * x, axis=-1, keepdims=True) + NORM_EPS) * g


def _inproj_a_kernel(x_ref, g_ref, wat_ref, crow_ref, srow_ref, ccol_ref, scol_ref, expand_ref,
                     qn_ref, kn_ref, oat_ref, xn_ref):
    tm = x_ref.shape[0]
    xn = _rms(x_ref[...], g_ref[...]).astype(BF16)
    xn_ref[...] = xn
    acc = lax.dot_general(wat_ref[...], xn, (((1,), (1,)), ((), ())),
                          preferred_element_type=F32)

    def per_token(row_ref, col_ref):
        return jnp.dot(row_ref[...], expand_ref[...], preferred_element_type=F32) + col_ref[...]

    cos = per_token(crow_ref, ccol_ref)
    sin = per_token(srow_ref, scol_ref)
    half = HEAD_DIM // 2
    reps = tm // HEAD_DIM
    for hh in range(A_COLS // HEAD_DIM):
        blk = acc[hh * HEAD_DIM:(hh + 1) * HEAD_DIM, :]
        if hh < VA_COL:
            gain = jnp.tile(qn_ref[...] if hh < KA_COL else kn_ref[...], (1, reps))
            ssq = jnp.sum(blk * blk, axis=0, keepdims=True)
            y = blk * lax.rsqrt(ssq * (1.0 / HEAD_DIM) + NORM_EPS) * gain
            partner = jnp.concatenate([y[half:, :], y[:half, :]], axis=0)
            blk = y * cos + partner * sin
        oat_ref[hh * HEAD_DIM:(hh + 1) * HEAD_DIM, :] = blk.astype(BF16)


def _inproj_a(x, g, wat, rope, qn, kn, *, tm=1024):
    s_len, d = x.shape
    assert s_len % tm == 0 and tm % GRID_W == 0 and wat.shape == (A_COLS, d)
    crow, srow, ccol, scol, expand = rope
    tile_rows = tm // GRID_W
    const = lambda i: (0, 0)
    return pl.pallas_call(
        _inproj_a_kernel,
        grid=(s_len // tm,),
        in_specs=[
            pl.BlockSpec((tm, d), lambda i: (i, 0)),
            pl.BlockSpec((1, d), const),
            pl.BlockSpec(wat.shape, const, pipeline_mode=pl.Buffered(1)),
            pl.BlockSpec((None, HEAD_DIM, 2 * tile_rows), lambda i: (i, 0, 0)),
            pl.BlockSpec((None, HEAD_DIM, 2 * tile_rows), lambda i: (i, 0, 0)),
            pl.BlockSpec((HEAD_DIM, tm), const),
            pl.BlockSpec((HEAD_DIM, tm), const),
            pl.BlockSpec((2 * tile_rows, tm), const),
            pl.BlockSpec((HEAD_DIM, HEAD_DIM), const),
            pl.BlockSpec((HEAD_DIM, HEAD_DIM), const),
        ],
        out_specs=[pl.BlockSpec((A_COLS, tm), lambda i: (0, i)),
                   pl.BlockSpec((tm, d), lambda i: (i, 0))],
        out_shape=[jax.ShapeDtypeStruct((A_COLS, s_len), BF16),
                   jax.ShapeDtypeStruct((s_len, d), BF16)],
        compiler_params=pltpu.CompilerParams(
            dimension_semantics=("parallel",), vmem_limit_bytes=VMEM_LIMIT),
        name="inproj_a",
    )(x, g, wat, crow, srow, ccol, scol, expand, qn, kn)


def _inproj_b_kernel(xn_ref, w_ref, o_ref):
    j = pl.program_id(1)

    @pl.when(j == 0)
    def _():
        nq = B_HEADS * HEAD_DIM
        acc = jnp.dot(xn_ref[...], w_ref[...], preferred_element_type=F32)
        o_ref[:, :nq] = (acc[:, :nq] * (SM_SCALE * LOG2E)).astype(BF16)
        o_ref[:, nq:] = acc[:, nq:].astype(BF16)

    @pl.when(j > 0)
    def _():
        o_ref[...] = jnp.dot(xn_ref[...], w_ref[...], preferred_element_type=F32).astype(BF16)


def _inproj_b(xn, w, *, tm=1024, tn=A_COLS):
    s_len, d = xn.shape
    n = w.shape[1] - A_COLS
    assert s_len % tm == 0 and n % tn == 0 and A_COLS % tn == 0 and tn >= B_HEADS * HEAD_DIM
    skip = A_COLS // tn
    return pl.pallas_call(
        _inproj_b_kernel,
        grid=(s_len // tm, n // tn),
        in_specs=[
            pl.BlockSpec((tm, d), lambda i, j: (i, 0)),
            pl.BlockSpec((d, tn), lambda i, j: (0, j + skip)),
        ],
        out_specs=pl.BlockSpec((tm, tn), lambda i, j: (i, j)),
        out_shape=jax.ShapeDtypeStruct((s_len, n), BF16),
        compiler_params=pltpu.CompilerParams(
            dimension_semantics=("parallel", "arbitrary"), vmem_limit_bytes=VMEM_LIMIT),
        name="inproj_b",
    )(xn, w)


EXP2_SAFE_BOUND = 60.0


ONES_ROWS = 16


def _attn_a_kernel(*refs, tq, tk, nk, n_cast):
    q_ref, kt_ref, vt_ref = refs[:3]
    w32_refs = refs[3:3 + n_cast]
    o_ref = refs[3 + n_cast]
    w16_refs = refs[4 + n_cast:4 + 2 * n_cast]
    qt_ref, kn_ref, vxt_ref, kmax_ref, m_ref, acc_ref = refs[4 + 2 * n_cast:]
    i = pl.program_id(1)
    for w32, w16 in zip(w32_refs, w16_refs):
        w16[...] = w32[...].astype(BF16)

    @pl.when(i == 0)
    def _prepare_kv():
        row = lax.broadcasted_iota(jnp.int32, (ONES_ROWS, tk), 0)
        ones_rows = jnp.where(row == 0, 1.0, 0.0).astype(BF16)
        kmax = jnp.zeros((1, tk), F32)
        for c in range(nk):
            cols = slice(c * tk, (c + 1) * tk)
            ktf = kt_ref[:, cols].astype(F32)
            kn_ref[cols, :] = ktf.T.astype(BF16)
            vxt_ref[c, :HEAD_DIM, :] = vt_ref[:, cols]
            vxt_ref[c, HEAD_DIM:, :] = ones_rows
            kmax = jnp.maximum(kmax, jnp.sum(ktf * ktf, axis=0, keepdims=True))
        kmax_ref[0] = jnp.max(kmax)

    for h in range(A_GROUP):
        qt_ref[:, h * tq:(h + 1) * tq] = q_ref[h * HEAD_DIM:(h + 1) * HEAD_DIM, :]
    acc_ref[...] = jnp.zeros(acc_ref.shape, F32)
    qf = qt_ref[...].astype(F32)
    qmax = jnp.max(jnp.sum(qf * qf, axis=0, keepdims=True))
    no_max_needed = qmax * kmax_ref[0] <= EXP2_SAFE_BOUND * EXP2_SAFE_BOUND

    @pl.when(no_max_needed)
    def _plain():
        def body(j, carry):
            off = pl.multiple_of(j * tk, tk)
            s = jnp.dot(kn_ref[pl.ds(off, tk), :], qt_ref[...], preferred_element_type=F32)
            p = jnp.exp2(s).astype(BF16)
            acc_ref[...] += jnp.dot(vxt_ref[j], p, preferred_element_type=F32)
            return carry

        lax.fori_loop(0, nk, body, 0, unroll=16)

    @pl.when(jnp.logical_not(no_max_needed))
    def _online():
        m_ref[...] = jnp.full(m_ref.shape, -jnp.inf, F32)

        def body(j, carry):
            off = pl.multiple_of(j * tk, tk)
            s = jnp.dot(kn_ref[pl.ds(off, tk), :], qt_ref[...], preferred_element_type=F32)
            m_prev = m_ref[...]
            m_new = jnp.maximum(m_prev, jnp.max(s, axis=0, keepdims=True))
            p = jnp.exp2(s - m_new).astype(BF16)
            acc_ref[...] = jnp.exp2(m_prev - m_new) * acc_ref[...] + jnp.dot(
                vxt_ref[j], p, preferred_element_type=F32)
            m_ref[...] = m_new
            return carry

        lax.fori_loop(0, nk, body, 0)

    out_t = acc_ref[:HEAD_DIM, :] / acc_ref[HEAD_DIM:HEAD_DIM + 1, :]
    for h in range(A_GROUP):
        o_ref[:, h * HEAD_DIM:(h + 1) * HEAD_DIM] = out_t[:, h * tq:(h + 1) * tq].T.astype(BF16)


def _attn_a(proj_t, cast_ws, *, tq=512, tk=512):
    s_len = proj_t.shape[1]
    assert s_len % tq == 0 and s_len % tk == 0
    m_cols = A_GROUP * tq
    gw = A_GROUP * HEAD_DIM
    nk = s_len // tk
    n_i = s_len // tq
    n_steps = A_KV_HEADS * n_i
    bf16_rows = 16
    slabs = []
    for w in cast_ws:
        assert w.shape[0] % (n_steps * bf16_rows) == 0
        slabs.append(pl.BlockSpec((w.shape[0] // n_steps, w.shape[1]),
                                  lambda g, i: (g * n_i + i, 0)))
    kern = functools.partial(_attn_a_kernel, tq=tq, tk=tk, nk=nk, n_cast=len(cast_ws))
    outs = pl.pallas_call(
        kern,
        grid=(A_KV_HEADS, n_i),
        in_specs=[
            pl.BlockSpec((gw, tq), lambda g, i: (g, i)),
            pl.BlockSpec((HEAD_DIM, s_len), lambda g, i: (KA_COL + g, 0)),
            pl.BlockSpec((HEAD_DIM, s_len), lambda g, i: (VA_COL + g, 0)),
        ] + slabs,
        out_specs=[pl.BlockSpec((tq, gw), lambda g, i: (i, g))] + slabs,
        out_shape=[jax.ShapeDtypeStruct((s_len, A_Q_HEADS * HEAD_DIM), BF16)]
        + [jax.ShapeDtypeStruct(w.shape, BF16) for w in cast_ws],
        scratch_shapes=[
            pltpu.VMEM((HEAD_DIM, m_cols), BF16),
            pltpu.VMEM((s_len, HEAD_DIM), BF16),
            pltpu.VMEM((nk, HEAD_DIM + ONES_ROWS, tk), BF16),
            pltpu.SMEM((1,), F32),
            pltpu.VMEM((1, m_cols), F32),
            pltpu.VMEM((HEAD_DIM + ONES_ROWS, m_cols), F32),
        ],
        compiler_params=pltpu.CompilerParams(
            dimension_semantics=("arbitrary", "arbitrary"), vmem_limit_bytes=VMEM_LIMIT),
        name="attn_global",
    )(proj_t, proj_t, proj_t, *cast_ws)
    return outs[0], tuple(outs[1:])


def _nb_block_types(rows):
    last_q = rows - NB_QROWS
    return ((0, 0), (2 * NB_QROWS, 2 * NB_QROWS - WIN_ROWS // 2), (last_q, rows - NB_KROWS))


def _attn_b_kernel(rpb_ref, q_ref, k_ref, v_ref, o_ref, tcol_ref, tab_ref, vx_ref, *, rows, ub):
    h = pl.program_id(0)
    step = pl.program_id(1)
    nblk = rows // NB_QROWS
    n_dr = 2 * WIN_ROWS - 1
    n_dc = 2 * WIN_COLS - 1

    @pl.when(step == 0)
    def _build_tables():
        chunk = NB_TQ
        lane_v = lax.broadcasted_iota(jnp.int32, (chunk, HEAD_DIM), 1)
        ones_col = jnp.where(lane_v == 0, 1.0, 0.0).astype(BF16)

        def copy_v(c, carry):
            off_c = pl.multiple_of(c * chunk, chunk)
            vx_ref[pl.ds(off_c, chunk), :HEAD_DIM] = v_ref[pl.ds(off_c, chunk), :]
            vx_ref[pl.ds(off_c, chunk), HEAD_DIM:] = ones_col
            return carry

        lax.fori_loop(0, v_ref.shape[0] // chunk, copy_v, 0)

        c = lax.broadcasted_iota(jnp.int32, (GRID_W, 2 * GRID_W), 0)
        kc = lax.broadcasted_iota(jnp.int32, (GRID_W, 2 * GRID_W), 1) % GRID_W
        c0 = jnp.clip(c - WIN_COLS // 2, 0, GRID_W - WIN_COLS)
        col_ok = (kc >= c0) & (kc < c0 + WIN_COLS)
        dc = kc - c + (WIN_COLS - 1)
        base = h * (n_dr * n_dc)

        def row_body(a, carry):
            t = jnp.full((GRID_W, 2 * GRID_W), NEG_BIG, F32)
            for b in range(n_dc):
                t = jnp.where(dc == b, rpb_ref[base + a * n_dc + b] * LOG2E, t)
            tcol_ref[a] = jnp.where(col_ok, t, NEG_BIG)
            return carry

        lax.fori_loop(0, n_dr, row_body, 0)

        lane = lax.broadcasted_iota(jnp.int32, (GRID_W, 2 * GRID_W), 1)
        neg = jnp.full((GRID_W, 2 * GRID_W), NEG_BIG, F32)
        for t, (r_first, k_first) in enumerate(_nb_block_types(rows)):
            for qr in range(NB_QROWS):
                r = r_first + qr
                r0 = min(max(r - WIN_ROWS // 2, 0), rows - WIN_ROWS)
                for jj in range(NB_KROWS // 2):
                    halves = []
                    for kr in (2 * jj, 2 * jj + 1):
                        k_abs = k_first + kr
                        ok = r0 <= k_abs < r0 + WIN_ROWS
                        halves.append(tcol_ref[k_abs - r + WIN_ROWS - 1] if ok else neg)
                    tab_ref[t, qr * GRID_W:(qr + 1) * GRID_W,
                            jj * 2 * GRID_W:(jj + 1) * 2 * GRID_W] = jnp.where(
                                lane < GRID_W, halves[0], halves[1])

    for u in range(ub):
        pb = step * ub + u
        btype = jnp.where(pb == 0, 0, jnp.where(pb == nblk - 1, 2, 1))
        k_first = jnp.clip(pb * NB_QROWS - WIN_ROWS // 2, 0, rows - NB_KROWS)
        off = pl.multiple_of(k_first * GRID_W, GRID_W)
        ks = k_ref[pl.ds(off, NB_TK), :]
        s = lax.dot_general(q_ref[u * NB_TQ:(u + 1) * NB_TQ, :], ks, (((1,), (1,)), ((), ())),
                            preferred_element_type=F32)
        s = s + tab_ref[btype]
        m = jnp.max(s, axis=1, keepdims=True)
        p = jnp.exp2(s - m).astype(BF16)
        acc = jnp.dot(p, vx_ref[pl.ds(off, NB_TK), :], preferred_element_type=F32)
        o_ref[u * NB_TQ:(u + 1) * NB_TQ, :] = (
            acc[:, :HEAD_DIM] / acc[:, HEAD_DIM:HEAD_DIM + 1]).astype(BF16)


def _attn_b(proj, rpb_flat, *, max_ub=16):
    s_len = proj.shape[0]
    rows = s_len // GRID_W
    nblk = rows // NB_QROWS
    assert rows % NB_QROWS == 0 and rows >= 3 * NB_QROWS + WIN_ROWS
    ub = max(u for u in range(1, max_ub + 1) if nblk % u == 0)
    kern = functools.partial(_attn_b_kernel, rows=rows, ub=ub)
    return pl.pallas_call(
        kern,
        grid=(B_HEADS, nblk // ub),
        in_specs=[
            pl.BlockSpec(memory_space=pltpu.SMEM),
            pl.BlockSpec((ub * NB_TQ, HEAD_DIM), lambda h, p: (p, QB_COL + h)),
            pl.BlockSpec((s_len, HEAD_DIM), lambda h, p: (0, KB_COL + h)),
            pl.BlockSpec((s_len, HEAD_DIM), lambda h, p: (0, VB_COL + h)),
        ],
        out_specs=pl.BlockSpec((ub * NB_TQ, HEAD_DIM), lambda h, p: (p, h)),
        out_shape=jax.ShapeDtypeStruct((s_len, B_HEADS * HEAD_DIM), BF16),
        scratch_shapes=[
            pltpu.VMEM((2 * WIN_ROWS - 1, GRID_W, 2 * GRID_W), F32),
            pltpu.VMEM((3, NB_TQ, NB_TK), F32),
            pltpu.VMEM((s_len, 2 * HEAD_DIM), BF16),
        ],
        compiler_params=pltpu.CompilerParams(
            dimension_semantics=("parallel", "arbitrary"), vmem_limit_bytes=VMEM_LIMIT),
        name="attn_nbr",
    )(rpb_flat, proj, proj, proj)


ROW_SUB = 256


def _row_subblocks(tm):
    return [slice(r, r + ROW_SUB) for r in range(0, tm, ROW_SUB)]


def _oproj_kernel(oa_ref, ob_ref, wo_ref, x_ref, g_ref, h_ref):
    na = oa_ref.shape[1]
    for rows in _row_subblocks(x_ref.shape[0]):
        mix = jnp.dot(oa_ref[rows, :], wo_ref[:na, :], preferred_element_type=F32)
        mix = mix + jnp.dot(ob_ref[rows, :], wo_ref[na:, :], preferred_element_type=F32)
        h_ref[rows, :] = x_ref[rows, :] + _rms(mix, g_ref[...])


def _oproj(oa, ob, wo, x, g, *, tm=512):
    s_len, d = x.shape
    assert s_len % tm == 0 and tm % ROW_SUB == 0
    return pl.pallas_call(
        _oproj_kernel,
        grid=(s_len // tm,),
        in_specs=[
            pl.BlockSpec((tm, oa.shape[1]), lambda i: (i, 0)),
            pl.BlockSpec((tm, ob.shape[1]), lambda i: (i, 0)),
            pl.BlockSpec(wo.shape, lambda i: (0, 0), pipeline_mode=pl.Buffered(1)),
            pl.BlockSpec((tm, d), lambda i: (i, 0)),
            pl.BlockSpec((1, d), lambda i: (0, 0)),
        ],
        out_specs=pl.BlockSpec((tm, d), lambda i: (i, 0)),
        out_shape=jax.ShapeDtypeStruct((s_len, d), F32),
        compiler_params=pltpu.CompilerParams(
            dimension_semantics=("parallel",), vmem_limit_bytes=VMEM_LIMIT),
        name="oproj",
    )(oa, ob, wo, x, g)


def _mlp_kernel(h_ref, gpre_ref, wup_ref, wdn_ref, gpost_ref, o_ref, xn_ref, acc_ref):
    f = pl.program_id(1)

    def ff_chunk():
        u = jnp.dot(xn_ref[...], wup_ref[...], preferred_element_type=F32)
        a = jnp.square(jnp.maximum(u, 0.0)).astype(BF16)
        return jnp.dot(a, wdn_ref[...], preferred_element_type=F32)

    @pl.when(f == 0)
    def _():
        xn_ref[...] = _rms(h_ref[...], gpre_ref[...]).astype(BF16)
        acc_ref[...] = ff_chunk()

    last = pl.num_programs(1) - 1

    @pl.when((f > 0) & (f < last))
    def _():
        acc_ref[...] += ff_chunk()

    @pl.when(f == last)
    def _():
        o_ref[...] = h_ref[...] + _rms(acc_ref[...] + ff_chunk(), gpost_ref[...])


def _mlp(h, gpre, wup, wdn, gpost, *, tm=512, tf=1024):
    s_len, d = h.shape
    d_ff = wup.shape[1]
    assert s_len % tm == 0 and d_ff % tf == 0
    return pl.pallas_call(
        _mlp_kernel,
        grid=(s_len // tm, d_ff // tf),
        in_specs=[
            pl.BlockSpec((tm, d), lambda i, f: (i, 0)),
            pl.BlockSpec((1, d), lambda i, f: (0, 0)),
            pl.BlockSpec((d, tf), lambda i, f: (0, f)),
            pl.BlockSpec((tf, d), lambda i, f: (f, 0)),
            pl.BlockSpec((1, d), lambda i, f: (0, 0)),
        ],
        out_specs=pl.BlockSpec((tm, d), lambda i, f: (i, 0)),
        out_shape=jax.ShapeDtypeStruct((s_len, d), F32),
        scratch_shapes=[pltpu.VMEM((tm, d), BF16), pltpu.VMEM((tm, d), F32)],
        compiler_params=pltpu.CompilerParams(
            dimension_semantics=("parallel", "arbitrary"), vmem_limit_bytes=VMEM_LIMIT),
        name="mlp",
    )(h, gpre, wup, wdn, gpost)


def _ple_kernel(h_ref, p_ref, gpre_ref, wg_ref, wp_ref, gpost_ref, o_ref):
    for rows in _row_subblocks(h_ref.shape[0]):
        h = h_ref[rows, :]
        xn = _rms(h, gpre_ref[...]).astype(BF16)
        gate = jax.nn.sigmoid(jnp.dot(xn, wg_ref[...], preferred_element_type=F32))
        e = jnp.dot(p_ref[rows, :].astype(BF16), wp_ref[...], preferred_element_type=F32) * gate
        o_ref[rows, :] = h + _rms(e, gpost_ref[...])


def _ple(h, p, gpre, wg, wp, gpost, *, tm=512):
    s_len, d = h.shape
    dp = p.shape[1]
    assert s_len % tm == 0 and tm % ROW_SUB == 0
    return pl.pallas_call(
        _ple_kernel,
        grid=(s_len // tm,),
        in_specs=[
            pl.BlockSpec((tm, d), lambda i: (i, 0)),
            pl.BlockSpec((tm, dp), lambda i: (i, 0)),
            pl.BlockSpec((1, d), lambda i: (0, 0)),
            pl.BlockSpec(wg.shape, lambda i: (0, 0), pipeline_mode=pl.Buffered(1)),
            pl.BlockSpec(wp.shape, lambda i: (0, 0), pipeline_mode=pl.Buffered(1)),
            pl.BlockSpec((1, d), lambda i: (0, 0)),
        ],
        out_specs=pl.BlockSpec((tm, d), lambda i: (i, 0)),
        out_shape=jax.ShapeDtypeStruct((s_len, d), F32),
        compiler_params=pltpu.CompilerParams(
            dimension_semantics=("parallel",), vmem_limit_bytes=VMEM_LIMIT),
        name="ple",
    )(h, p, gpre, wg, wp, gpost)


def _pair_major(a):
    lead = a.shape[:-1]
    return a.reshape(lead + (2, 2, HEAD_DIM // 4)).swapaxes(-3, -2).reshape(lead + (HEAD_DIM,))


def _rope_tables(s_len, tm):
    rows = s_len // GRID_W
    tile_rows = tm // GRID_W
    n_freq = HEAD_DIM // 4
    freqs = ROPE_THETA ** (-jnp.arange(n_freq, dtype=F32) / n_freq)
    ar = freqs[:, None] * jnp.arange(rows, dtype=F32)[None, :]
    ac = freqs[:, None] * jnp.arange(GRID_W, dtype=F32)[None, :]
    zr, zc = jnp.zeros_like(ar), jnp.zeros_like(ac)

    def row_table(t):
        hi = t.astype(BF16)
        lo = (t - hi.astype(F32)).astype(BF16)
        per_tile = lambda a: a.reshape(HEAD_DIM, rows // tile_rows, tile_rows).swapaxes(0, 1)
        return jnp.concatenate([per_tile(hi), per_tile(lo)], axis=-1)

    crow = row_table(jnp.concatenate([jnp.cos(ar), zr, jnp.cos(ar), zr], axis=0))
    srow = row_table(jnp.concatenate([-jnp.sin(ar), zr, jnp.sin(ar), zr], axis=0))
    ccol = jnp.tile(jnp.concatenate([zc, jnp.cos(ac), zc, jnp.cos(ac)], axis=0), (1, tile_rows))
    scol = jnp.tile(jnp.concatenate([zc, -jnp.sin(ac), zc, jnp.sin(ac)], axis=0), (1, tile_rows))
    tok_row = jnp.arange(tm, dtype=jnp.int32)[None, :] // GRID_W
    line = jnp.arange(2 * tile_rows, dtype=jnp.int32)[:, None] % tile_rows
    expand = (tok_row == line).astype(BF16)
    return crow, srow, ccol, scol, expand


def kernel(x, p, pre_mix_norm, w_in, q_norm, k_norm, rel_pos_bias, w_o, post_mix_norm,
           pre_mlp_norm, w_up, w_down, post_mlp_norm, pre_ple_norm, w_ple_gate, w_ple_proj,
           post_ple_norm):
    b, s_len, d = x.shape
    depth = w_in.shape[0]
    tm_in = 1024
    rope = _rope_tables(s_len, tm_in)
    outs = []
    for bi in range(b):
        h = x[bi]
        for i in range(depth):
            lanes = (HEAD_DIM, HEAD_DIM)
            qn = jnp.broadcast_to(_pair_major(q_norm[i] * (SM_SCALE * LOG2E))[:, None], lanes)
            kn = jnp.broadcast_to(_pair_major(k_norm[i])[:, None], lanes)
            n_rot = VA_COL * HEAD_DIM
            w16 = w_in[i].astype(BF16)
            w_rot = _pair_major(w16[:, :n_rot].reshape(d, VA_COL, HEAD_DIM)).reshape(d, n_rot)
            wat = jnp.concatenate([w_rot, w16[:, n_rot:A_COLS]], axis=1).T
            proj_t, xn = _inproj_a(h, pre_mix_norm[i].reshape(1, d), wat, rope, qn, kn, tm=tm_in)
            proj = _inproj_b(xn, w16)
            out_a, (wo16, wup16, wdn16, wg16) = _attn_a(
                proj_t, (w_o[i], w_up[i], w_down[i], w_ple_gate[i]))
            out_b = _attn_b(proj, rel_pos_bias[i].reshape(-1))
            h = _oproj(out_a, out_b, wo16, h, post_mix_norm[i].reshape(1, d))
            h = _mlp(h, pre_mlp_norm[i].reshape(1, d), wup16, wdn16, post_mlp_norm[i].reshape(1, d))
            h = _ple(h, p[i, bi], pre_ple_norm[i].reshape(1, d), wg16,
                     w_ple_proj[i].astype(BF16), post_ple_norm[i].reshape(1, d))
        outs.append(h)
    return jnp.stack(outs, axis=0)
```

```python
import functools
import math

import jax
import jax.numpy as jnp
from jax import lax
from jax.experimental import pallas as pl
from jax.experimental.pallas import tpu as pltpu

F32 = jnp.float32
BF16 = jnp.bfloat16

HEAD_DIM = 128
GRID_W = 64
A_Q_HEADS = 8
A_KV_HEADS = 2
A_GROUP = A_Q_HEADS // A_KV_HEADS
B_HEADS = 8
WIN_ROWS = 8
WIN_COLS = 16
ROPE_THETA = 10000.0
NORM_EPS = 1e-6
LOG2E = math.log2(math.e)
SM_SCALE = 1.0 / math.sqrt(HEAD_DIM)
NEG_BIG = -1e30

QA_COL, KA_COL, VA_COL = 0, 8, 10
A_COLS = 12 * HEAD_DIM
QB_COL, KB_COL, VB_COL = 0, 8, 16

V7X_VMEM_BYTES = 64 * 1024 * 1024
VMEM_LIMIT = V7X_VMEM_BYTES - 8 * 1024 * 1024

NB_QROWS = 4
NB_KROWS = NB_QROWS + WIN_ROWS
NB_TQ = NB_QROWS * GRID_W
NB_TK = NB_KROWS * GRID_W


def _rms(x, g):
    return x * lax.rsqrt(jnp.mean(x * x, axis=-1, keepdims=True) + NORM_EPS) * g


def _inproj_a_kernel(x_ref, g_ref, wat_ref, crow_ref, srow_ref, ccol_ref, scol_ref, expand_ref,
                     qn_ref, kn_ref, oat_ref, xn_ref):
    tm = x_ref.shape[0]
    xn = _rms(x_ref[...], g_ref[...]).astype(BF16)
    xn_ref[...] = xn
    acc = lax.dot_general(wat_ref[...], xn, (((1,), (1,)), ((), ())),
                          preferred_element_type=F32)

    def per_token(row_ref, col_ref):
        return jnp.dot(row_ref[...], expand_ref[...], preferred_element_type=F32) + col_ref[...]

    cos = per_token(crow_ref, ccol_ref)
    sin = per_token(srow_ref, scol_ref)
    q4 = HEAD_DIM // 4
    reps = tm // HEAD_DIM
    for hh in range(A_COLS // HEAD_DIM):
        blk = acc[hh * HEAD_DIM:(hh + 1) * HEAD_DIM, :]
        if hh < VA_COL:
            gain = jnp.tile(qn_ref[...] if hh < KA_COL else kn_ref[...], (1, reps))
            ssq = jnp.sum(blk * blk, axis=0, keepdims=True)
            y = blk * lax.rsqrt(ssq * (1.0 / HEAD_DIM) + NORM_EPS) * gain
            partner = jnp.concatenate(
                [y[q4:2 * q4, :], y[:q4, :], y[3 * q4:, :], y[2 * q4:3 * q4, :]], axis=0)
            blk = y * cos + partner * sin
        oat_ref[hh * HEAD_DIM:(hh + 1) * HEAD_DIM, :] = blk.astype(BF16)


def _inproj_a(x, g, wat, rope, qn, kn, *, tm=1024):
    s_len, d = x.shape
    assert s_len % tm == 0 and tm % GRID_W == 0 and wat.shape == (A_COLS, d)
    crow, srow, ccol, scol, expand = rope
    tile_rows = tm // GRID_W
    const = lambda i: (0, 0)
    return pl.pallas_call(
        _inproj_a_kernel,
        grid=(s_len // tm,),
        in_specs=[
            pl.BlockSpec((tm, d), lambda i: (i, 0)),
            pl.BlockSpec((1, d), const),
            pl.BlockSpec(wat.shape, const, pipeline_mode=pl.Buffered(1)),
            pl.BlockSpec((None, HEAD_DIM, 2 * tile_rows), lambda i: (i, 0, 0)),
            pl.BlockSpec((None, HEAD_DIM, 2 * tile_rows), lambda i: (i, 0, 0)),
            pl.BlockSpec((HEAD_DIM, tm), const),
            pl.BlockSpec((HEAD_DIM, tm), const),
            pl.BlockSpec((2 * tile_rows, tm), const),
            pl.BlockSpec((HEAD_DIM, HEAD_DIM), const),
            pl.BlockSpec((HEAD_DIM, HEAD_DIM), const),
        ],
        out_specs=[pl.BlockSpec((A_COLS, tm), lambda i: (0, i)),
                   pl.BlockSpec((tm, d), lambda i: (i, 0))],
        out_shape=[jax.ShapeDtypeStruct((A_COLS, s_len), BF16),
                   jax.ShapeDtypeStruct((s_len, d), BF16)],
        compiler_params=pltpu.CompilerParams(
            dimension_semantics=("parallel",), vmem_limit_bytes=VMEM_LIMIT),
        name="inproj_a",
    )(x, g, wat, crow, srow, ccol, scol, expand, qn, kn)


def _inproj_b_kernel(xn_ref, w_ref, o_ref):
    j = pl.program_id(1)

    @pl.when(j == 0)
    def _():
        nq = B_HEADS * HEAD_DIM
        acc = jnp.dot(xn_ref[...], w_ref[...], preferred_element_type=F32)
        o_ref[:, :nq] = (acc[:, :nq] * (SM_SCALE * LOG2E)).astype(BF16)
        o_ref[:, nq:] = acc[:, nq:].astype(BF16)

    @pl.when(j > 0)
    def _():
        o_ref[...] = jnp.dot(xn_ref[...], w_ref[...], preferred_element_type=F32).astype(BF16)


def _inproj_b(xn, w, *, tm=1024, tn=A_COLS):
    s_len, d = xn.shape
    n = w.shape[1] - A_COLS
    assert s_len % tm == 0 and n % tn == 0 and A_COLS % tn == 0 and tn >= B_HEADS * HEAD_DIM
    skip = A_COLS // tn
    return pl.pallas_call(
        _inproj_b_kernel,
        grid=(s_len // tm, n // tn),
        in_specs=[
            pl.BlockSpec((tm, d), lambda i, j: (i, 0)),
            pl.BlockSpec((d, tn), lambda i, j: (0, j + skip)),
        ],
        out_specs=pl.BlockSpec((tm, tn), lambda i, j: (i, j)),
        out_shape=jax.ShapeDtypeStruct((s_len, n), BF16),
        compiler_params=pltpu.CompilerParams(
            dimension_semantics=("parallel", "arbitrary"), vmem_limit_bytes=VMEM_LIMIT),
        name="inproj_b",
    )(xn, w)


EXP2_SAFE_BOUND = 60.0


ONES_ROWS = 16


def _attn_a_kernel(*refs, tq, tk, nk, n_cast):
    q_ref, kt_ref, vt_ref = refs[:3]
    w32_refs = refs[3:3 + n_cast]
    o_ref = refs[3 + n_cast]
    w16_refs = refs[4 + n_cast:4 + 2 * n_cast]
    qt_ref, kn_ref, vxt_ref, kmax_ref, m_ref, acc_ref = refs[4 + 2 * n_cast:]
    i = pl.program_id(1)
    for w32, w16 in zip(w32_refs, w16_refs):
        w16[...] = w32[...].astype(BF16)

    @pl.when(i == 0)
    def _prepare_kv():
        row = lax.broadcasted_iota(jnp.int32, (ONES_ROWS, tk), 0)
        ones_rows = jnp.where(row == 0, 1.0, 0.0).astype(BF16)
        kmax = jnp.zeros((1, tk), F32)
        for c in range(nk):
            cols = slice(c * tk, (c + 1) * tk)
            ktf = kt_ref[:, cols].astype(F32)
            kn_ref[cols, :] = ktf.T.astype(BF16)
            vxt_ref[c, :HEAD_DIM, :] = vt_ref[:, cols]
            vxt_ref[c, HEAD_DIM:, :] = ones_rows
            kmax = jnp.maximum(kmax, jnp.sum(ktf * ktf, axis=0, keepdims=True))
        kmax_ref[0] = jnp.max(kmax)

    for h in range(A_GROUP):
        qt_ref[:, h * tq:(h + 1) * tq] = q_ref[h * HEAD_DIM:(h + 1) * HEAD_DIM, :]
    acc_ref[...] = jnp.zeros(acc_ref.shape, F32)
    qf = qt_ref[...].astype(F32)
    qmax = jnp.max(jnp.sum(qf * qf, axis=0, keepdims=True))
    no_max_needed = qmax * kmax_ref[0] <= EXP2_SAFE_BOUND * EXP2_SAFE_BOUND

    @pl.when(no_max_needed)
    def _plain():
        def body(j, carry):
            off = pl.multiple_of(j * tk, tk)
            s = jnp.dot(kn_ref[pl.ds(off, tk), :], qt_ref[...], preferred_element_type=F32)
            p = jnp.exp2(s).astype(BF16)
            acc_ref[...] += jnp.dot(vxt_ref[j], p, preferred_element_type=F32)
            return carry

        lax.fori_loop(0, nk, body, 0, unroll=16)

    @pl.when(jnp.logical_not(no_max_needed))
    def _online():
        m_ref[...] = jnp.full(m_ref.shape, -jnp.inf, F32)

        def body(j, carry):
            off = pl.multiple_of(j * tk, tk)
            s = jnp.dot(kn_ref[pl.ds(off, tk), :], qt_ref[...], preferred_element_type=F32)
            m_prev = m_ref[...]
            m_new = jnp.maximum(m_prev, jnp.max(s, axis=0, keepdims=True))
            p = jnp.exp2(s - m_new).astype(BF16)
            acc_ref[...] = jnp.exp2(m_prev - m_new) * acc_ref[...] + jnp.dot(
                vxt_ref[j], p, preferred_element_type=F32)
            m_ref[...] = m_new
            return carry

        lax.fori_loop(0, nk, body, 0)

    out_t = acc_ref[:HEAD_DIM, :] / acc_ref[HEAD_DIM:HEAD_DIM + 1, :]
    for h in range(A_GROUP):
        o_ref[:, h * HEAD_DIM:(h + 1) * HEAD_DIM] = out_t[:, h * tq:(h + 1) * tq].T.astype(BF16)


def _attn_a(proj_t, cast_ws, *, tq=512, tk=512):
    s_len = proj_t.shape[1]
    assert s_len % tq == 0 and s_len % tk == 0
    m_cols = A_GROUP * tq
    gw = A_GROUP * HEAD_DIM
    nk = s_len // tk
    n_i = s_len // tq
    n_steps = A_KV_HEADS * n_i
    bf16_rows = 16
    slabs = []
    for w in cast_ws:
        assert w.shape[0] % (n_steps * bf16_rows) == 0
        slabs.append(pl.BlockSpec((w.shape[0] // n_steps, w.shape[1]),
                                  lambda g, i: (g * n_i + i, 0)))
    kern = functools.partial(_attn_a_kernel, tq=tq, tk=tk, nk=nk, n_cast=len(cast_ws))
    outs = pl.pallas_call(
        kern,
        grid=(A_KV_HEADS, n_i),
        in_specs=[
            pl.BlockSpec((gw, tq), lambda g, i: (g, i)),
            pl.BlockSpec((HEAD_DIM, s_len), lambda g, i: (KA_COL + g, 0)),
            pl.BlockSpec((HEAD_DIM, s_len), lambda g, i: (VA_COL + g, 0)),
        ] + slabs,
        out_specs=[pl.BlockSpec((tq, gw), lambda g, i: (i, g))] + slabs,
        out_shape=[jax.ShapeDtypeStruct((s_len, A_Q_HEADS * HEAD_DIM), BF16)]
        + [jax.ShapeDtypeStruct(w.shape, BF16) for w in cast_ws],
        scratch_shapes=[
            pltpu.VMEM((HEAD_DIM, m_cols), BF16),
            pltpu.VMEM((s_len, HEAD_DIM), BF16),
            pltpu.VMEM((nk, HEAD_DIM + ONES_ROWS, tk), BF16),
            pltpu.SMEM((1,), F32),
            pltpu.VMEM((1, m_cols), F32),
            pltpu.VMEM((HEAD_DIM + ONES_ROWS, m_cols), F32),
        ],
        compiler_params=pltpu.CompilerParams(
            dimension_semantics=("arbitrary", "arbitrary"), vmem_limit_bytes=VMEM_LIMIT),
        name="attn_global",
    )(proj_t, proj_t, proj_t, *cast_ws)
    return outs[0], tuple(outs[1:])


def _nb_block_types(rows):
    last_q = rows - NB_QROWS
    return ((0, 0), (2 * NB_QROWS, 2 * NB_QROWS - WIN_ROWS // 2), (last_q, rows - NB_KROWS))


def _attn_b_kernel(rpb_ref, q_ref, k_ref, v_ref, o_ref, tcol_ref, tab_ref, vx_ref, *, rows, ub):
    h = pl.program_id(0)
    step = pl.program_id(1)
    nblk = rows // NB_QROWS
    n_dr = 2 * WIN_ROWS - 1
    n_dc = 2 * WIN_COLS - 1

    @pl.when(step == 0)
    def _build_tables():
        chunk = NB_TQ
        lane_v = lax.broadcasted_iota(jnp.int32, (chunk, HEAD_DIM), 1)
        ones_col = jnp.where(lane_v == 0, 1.0, 0.0).astype(BF16)

        def copy_v(c, carry):
            off_c = pl.multiple_of(c * chunk, chunk)
            vx_ref[pl.ds(off_c, chunk), :HEAD_DIM] = v_ref[pl.ds(off_c, chunk), :]
            vx_ref[pl.ds(off_c, chunk), HEAD_DIM:] = ones_col
            return carry

        lax.fori_loop(0, v_ref.shape[0] // chunk, copy_v, 0)

        c = lax.broadcasted_iota(jnp.int32, (GRID_W, 2 * GRID_W), 0)
        kc = lax.broadcasted_iota(jnp.int32, (GRID_W, 2 * GRID_W), 1) % GRID_W
        c0 = jnp.clip(c - WIN_COLS // 2, 0, GRID_W - WIN_COLS)
        col_ok = (kc >= c0) & (kc < c0 + WIN_COLS)
        dc = kc - c + (WIN_COLS - 1)
        base = h * (n_dr * n_dc)

        def row_body(a, carry):
            t = jnp.full((GRID_W, 2 * GRID_W), NEG_BIG, F32)
            for b in range(n_dc):
                t = jnp.where(dc == b, rpb_ref[base + a * n_dc + b] * LOG2E, t)
            tcol_ref[a] = jnp.where(col_ok, t, NEG_BIG)
            return carry

        lax.fori_loop(0, n_dr, row_body, 0)

        lane = lax.broadcasted_iota(jnp.int32, (GRID_W, 2 * GRID_W), 1)
        neg = jnp.full((GRID_W, 2 * GRID_W), NEG_BIG, F32)
        for t, (r_first, k_first) in enumerate(_nb_block_types(rows)):
            for qr in range(NB_QROWS):
                r = r_first + qr
                r0 = min(max(r - WIN_ROWS // 2, 0), rows - WIN_ROWS)
                for jj in range(NB_KROWS // 2):
                    halves = []
                    for kr in (2 * jj, 2 * jj + 1):
                        k_abs = k_first + kr
                        ok = r0 <= k_abs < r0 + WIN_ROWS
                        halves.append(tcol_ref[k_abs - r + WIN_ROWS - 1] if ok else neg)
                    tab_ref[t, qr * GRID_W:(qr + 1) * GRID_W,
                            jj * 2 * GRID_W:(jj + 1) * 2 * GRID_W] = jnp.where(
                                lane < GRID_W, halves[0], halves[1])

    for u in range(ub):
        pb = step * ub + u
        btype = jnp.where(pb == 0, 0, jnp.where(pb == nblk - 1, 2, 1))
        k_first = jnp.clip(pb * NB_QROWS - WIN_ROWS // 2, 0, rows - NB_KROWS)
        off = pl.multiple_of(k_first * GRID_W, GRID_W)
        ks = k_ref[pl.ds(off, NB_TK), :]
        s = lax.dot_general(q_ref[u * NB_TQ:(u + 1) * NB_TQ, :], ks, (((1,), (1,)), ((), ())),
                            preferred_element_type=F32)
        s = s + tab_ref[btype]
        m = jnp.max(s, axis=1, keepdims=True)
        p = jnp.exp2(s - m).astype(BF16)
        acc = jnp.dot(p, vx_ref[pl.ds(off, NB_TK), :], preferred_element_type=F32)
        o_ref[u * NB_TQ:(u + 1) * NB_TQ, :] = (
            acc[:, :HEAD_DIM] / acc[:, HEAD_DIM:HEAD_DIM + 1]).astype(BF16)


def _attn_b(proj, rpb_flat, *, max_ub=16):
    s_len = proj.shape[0]
    rows = s_len // GRID_W
    nblk = rows // NB_QROWS
    assert rows % NB_QROWS == 0 and rows >= 3 * NB_QROWS + WIN_ROWS
    ub = max(u for u in range(1, max_ub + 1) if nblk % u == 0)
    kern = functools.partial(_attn_b_kernel, rows=rows, ub=ub)
    return pl.pallas_call(
        kern,
        grid=(B_HEADS, nblk // ub),
        in_specs=[
            pl.BlockSpec(memory_space=pltpu.SMEM),
            pl.BlockSpec((ub * NB_TQ, HEAD_DIM), lambda h, p: (p, QB_COL + h)),
            pl.BlockSpec((s_len, HEAD_DIM), lambda h, p: (0, KB_COL + h)),
            pl.BlockSpec((s_len, HEAD_DIM), lambda h, p: (0, VB_COL + h)),
        ],
        out_specs=pl.BlockSpec((ub * NB_TQ, HEAD_DIM), lambda h, p: (p, h)),
        out_shape=jax.ShapeDtypeStruct((s_len, B_HEADS * HEAD_DIM), BF16),
        scratch_shapes=[
            pltpu.VMEM((2 * WIN_ROWS - 1, GRID_W, 2 * GRID_W), F32),
            pltpu.VMEM((3, NB_TQ, NB_TK), F32),
            pltpu.VMEM((s_len, 2 * HEAD_DIM), BF16),
        ],
        compiler_params=pltpu.CompilerParams(
            dimension_semantics=("parallel", "arbitrary"), vmem_limit_bytes=VMEM_LIMIT),
        name="attn_nbr",
    )(rpb_flat, proj, proj, proj)


ROW_SUB = 256


def _row_subblocks(tm):
    return [slice(r, r + ROW_SUB) for r in range(0, tm, ROW_SUB)]


def _oproj_kernel(oa_ref, ob_ref, wo_ref, x_ref, g_ref, h_ref):
    na = oa_ref.shape[1]
    for rows in _row_subblocks(x_ref.shape[0]):
        mix = jnp.dot(oa_ref[rows, :], wo_ref[:na, :], preferred_element_type=F32)
        mix = mix + jnp.dot(ob_ref[rows, :], wo_ref[na:, :], preferred_element_type=F32)
        h_ref[rows, :] = x_ref[rows, :] + _rms(mix, g_ref[...])


def _oproj(oa, ob, wo, x, g, *, tm=512):
    s_len, d = x.shape
    assert s_len % tm == 0 and tm % ROW_SUB == 0
    return pl.pallas_call(
        _oproj_kernel,
        grid=(s_len // tm,),
        in_specs=[
            pl.BlockSpec((tm, oa.shape[1]), lambda i: (i, 0)),
            pl.BlockSpec((tm, ob.shape[1]), lambda i: (i, 0)),
            pl.BlockSpec(wo.shape, lambda i: (0, 0), pipeline_mode=pl.Buffered(1)),
            pl.BlockSpec((tm, d), lambda i: (i, 0)),
            pl.BlockSpec((1, d), lambda i: (0, 0)),
        ],
        out_specs=pl.BlockSpec((tm, d), lambda i: (i, 0)),
        out_shape=jax.ShapeDtypeStruct((s_len, d), F32),
        compiler_params=pltpu.CompilerParams(
            dimension_semantics=("parallel",), vmem_limit_bytes=VMEM_LIMIT),
        name="oproj",
    )(oa, ob, wo, x, g)


def _mlp_kernel(h_ref, gpre_ref, wup_ref, wdn_ref, gpost_ref, o_ref, xn_ref, acc_ref):
    f = pl.program_id(1)

    def ff_chunk():
        u = jnp.dot(xn_ref[...], wup_ref[...], preferred_element_type=F32)
        a = jnp.square(jnp.maximum(u, 0.0)).astype(BF16)
        return jnp.dot(a, wdn_ref[...], preferred_element_type=F32)

    @pl.when(f == 0)
    def _():
        xn_ref[...] = _rms(h_ref[...], gpre_ref[...]).astype(BF16)
        acc_ref[...] = ff_chunk()

    last = pl.num_programs(1) - 1

    @pl.when((f > 0) & (f < last))
    def _():
        acc_ref[...] += ff_chunk()

    @pl.when(f == last)
    def _():
        o_ref[...] = h_ref[...] + _rms(acc_ref[...] + ff_chunk(), gpost_ref[...])


def _mlp(h, gpre, wup, wdn, gpost, *, tm=512, tf=1024):
    s_len, d = h.shape
    d_ff = wup.shape[1]
    assert s_len % tm == 0 and d_ff % tf == 0
    return pl.pallas_call(
        _mlp_kernel,
        grid=(s_len // tm, d_ff // tf),
        in_specs=[
            pl.BlockSpec((tm, d), lambda i, f: (i, 0)),
            pl.BlockSpec((1, d), lambda i, f: (0, 0)),
            pl.BlockSpec((d, tf), lambda i, f: (0, f)),
            pl.BlockSpec((tf, d), lambda i, f: (f, 0)),
            pl.BlockSpec((1, d), lambda i, f: (0, 0)),
        ],
        out_specs=pl.BlockSpec((tm, d), lambda i, f: (i, 0)),
        out_shape=jax.ShapeDtypeStruct((s_len, d), F32),
        scratch_shapes=[pltpu.VMEM((tm, d), BF16), pltpu.VMEM((tm, d), F32)],
        compiler_params=pltpu.CompilerParams(
            dimension_semantics=("parallel", "arbitrary"), vmem_limit_bytes=VMEM_LIMIT),
        name="mlp",
    )(h, gpre, wup, wdn, gpost)


def _ple_kernel(h_ref, p_ref, gpre_ref, wg_ref, wp_ref, gpost_ref, o_ref):
    for rows in _row_subblocks(h_ref.shape[0]):
        h = h_ref[rows, :]
        xn = _rms(h, gpre_ref[...]).astype(BF16)
        gate = jax.nn.sigmoid(jnp.dot(xn, wg_ref[...], preferred_element_type=F32))
        e = jnp.dot(p_ref[rows, :].astype(BF16), wp_ref[...], preferred_element_type=F32) * gate
        o_ref[rows, :] = h + _rms(e, gpost_ref[...])


def _ple(h, p, gpre, wg, wp, gpost, *, tm=512):
    s_len, d = h.shape
    dp = p.shape[1]
    assert s_len % tm == 0 and tm % ROW_SUB == 0
    return pl.pallas_call(
        _ple_kernel,
        grid=(s_len // tm,),
        in_specs=[
            pl.BlockSpec((tm, d), lambda i: (i, 0)),
            pl.BlockSpec((tm, dp), lambda i: (i, 0)),
            pl.BlockSpec((1, d), lambda i: (0, 0)),
            pl.BlockSpec(wg.shape, lambda i: (0, 0), pipeline_mode=pl.Buffered(1)),
            pl.BlockSpec(wp.shape, lambda i: (0, 0), pipeline_mode=pl.Buffered(1)),
            pl.BlockSpec((1, d), lambda i: (0, 0)),
        ],
        out_specs=pl.BlockSpec((tm, d), lambda i: (i, 0)),
        out_shape=jax.ShapeDtypeStruct((s_len, d), F32),
        compiler_params=pltpu.CompilerParams(
            dimension_semantics=("parallel",), vmem_limit_bytes=VMEM_LIMIT),
        name="ple",
    )(h, p, gpre, wg, wp, gpost)


def _rope_tables(s_len, tm):
    rows = s_len // GRID_W
    tile_rows = tm // GRID_W
    n_freq = HEAD_DIM // 4
    freqs = ROPE_THETA ** (-jnp.arange(n_freq, dtype=F32) / n_freq)
    ar = freqs[:, None] * jnp.arange(rows, dtype=F32)[None, :]
    ac = freqs[:, None] * jnp.arange(GRID_W, dtype=F32)[None, :]
    zr, zc = jnp.zeros_like(ar), jnp.zeros_like(ac)

    def row_table(t):
        hi = t.astype(BF16)
        lo = (t - hi.astype(F32)).astype(BF16)
        per_tile = lambda a: a.reshape(HEAD_DIM, rows // tile_rows, tile_rows).swapaxes(0, 1)
        return jnp.concatenate([per_tile(hi), per_tile(lo)], axis=-1)

    crow = row_table(jnp.concatenate([jnp.cos(ar), jnp.cos(ar), zr, zr], axis=0))
    srow = row_table(jnp.concatenate([-jnp.sin(ar), jnp.sin(ar), zr, zr], axis=0))
    ccol = jnp.tile(jnp.concatenate([zc, zc, jnp.cos(ac), jnp.cos(ac)], axis=0), (1, tile_rows))
    scol = jnp.tile(jnp.concatenate([zc, zc, -jnp.sin(ac), jnp.sin(ac)], axis=0), (1, tile_rows))
    tok_row = jnp.arange(tm, dtype=jnp.int32)[None, :] // GRID_W
    line = jnp.arange(2 * tile_rows, dtype=jnp.int32)[:, None] % tile_rows
    expand = (tok_row == line).astype(BF16)
    return crow, srow, ccol, scol, expand


def kernel(x, p, pre_mix_norm, w_in, q_norm, k_norm, rel_pos_bias, w_o, post_mix_norm,
           pre_mlp_norm, w_up, w_down, post_mlp_norm, pre_ple_norm, w_ple_gate, w_ple_proj,
           post_ple_norm):
    b, s_len, d = x.shape
    depth = w_in.shape[0]
    tm_in = 1024
    rope = _rope_tables(s_len, tm_in)
    outs = []
    for bi in range(b):
        h = x[bi]
        for i in range(depth):
            lanes = (HEAD_DIM, HEAD_DIM)
            qn = jnp.broadcast_to((q_norm[i] * (SM_SCALE * LOG2E))[:, None], lanes)
            kn = jnp.broadcast_to(k_norm[i][:, None], lanes)
            w16 = w_in[i].astype(BF16)
            wat = w16[:, :A_COLS].T
            proj_t, xn = _inproj_a(h, pre_mix_norm[i].reshape(1, d), wat, rope, qn, kn, tm=tm_in)
            proj = _inproj_b(xn, w16)
            out_a, (wo16, wup16, wdn16, wg16) = _attn_a(
                proj_t, (w_o[i], w_up[i], w_down[i], w_ple_gate[i]))
            out_b = _attn_b(proj, rel_pos_bias[i].reshape(-1))
            h = _oproj(out_a, out_b, wo16, h, post_mix_norm[i].reshape(1, d))
            h = _mlp(h, pre_mlp_norm[i].reshape(1, d), wup16, wdn16, post_mlp_norm[i].reshape(1, d))
            h = _ple(h, p[i, bi], pre_ple_norm[i].reshape(1, d), wg16,
                     w_ple_proj[i].astype(BF16), post_ple_norm[i].reshape(1, d))
        outs.append(h)
    return jnp.stack(outs, axis=0)
```

```python
import functools
import math

import jax
import jax.numpy as jnp
from jax import lax
from jax.experimental import pallas as pl
from jax.experimental.pallas import tpu as pltpu

F32 = jnp.float32
BF16 = jnp.bfloat16

HEAD_DIM = 128
GRID_W = 64
A_Q_HEADS = 8
A_KV_HEADS = 2
A_GROUP = A_Q_HEADS // A_KV_HEADS
B_HEADS = 8
WIN_ROWS = 8
WIN_COLS = 16
ROPE_THETA = 10000.0
NORM_EPS = 1e-6
LOG2E = math.log2(math.e)
SM_SCALE = 1.0 / math.sqrt(HEAD_DIM)
NEG_BIG = -1e30

QA_COL, KA_COL, VA_COL = 0, 8, 10
A_COLS = 12 * HEAD_DIM
QB_COL, KB_COL, VB_COL = 0, 8, 16

V7X_VMEM_BYTES = 64 * 1024 * 1024
VMEM_LIMIT = V7X_VMEM_BYTES - 8 * 1024 * 1024

NB_QROWS = 4
NB_KROWS = NB_QROWS + WIN_ROWS
NB_TQ = NB_QROWS * GRID_W
NB_TK = NB_KROWS * GRID_W


def _rms(x, g):
    return x * lax.rsqrt(jnp.mean(x * x, axis=-1, keepdims=True) + NORM_EPS) * g


def _inproj_a_kernel(x_ref, g_ref, wat_ref, crow_ref, srow_ref, ccol_ref, scol_ref, expand_ref,
                     qn_ref, kn_ref, oat_ref, xn_ref):
    tm = x_ref.shape[0]
    xn = _rms(x_ref[...], g_ref[...]).astype(BF16)
    xn_ref[...] = xn
    acc = lax.dot_general(wat_ref[...], xn, (((1,), (1,)), ((), ())),
                          preferred_element_type=F32)

    def per_token(row_ref, col_ref):
        return jnp.dot(row_ref[...], expand_ref[...], preferred_element_type=F32) + col_ref[...]

    cos = per_token(crow_ref, ccol_ref)
    sin = per_token(srow_ref, scol_ref)
    q4 = HEAD_DIM // 4
    reps = tm // HEAD_DIM
    for hh in range(A_COLS // HEAD_DIM):
        blk = acc[hh * HEAD_DIM:(hh + 1) * HEAD_DIM, :]
        if hh < VA_COL:
            gain = jnp.tile(qn_ref[...] if hh < KA_COL else kn_ref[...], (1, reps))
            ssq = jnp.sum(blk * blk, axis=0, keepdims=True)
            y = blk * lax.rsqrt(ssq * (1.0 / HEAD_DIM) + NORM_EPS) * gain
            partner = jnp.concatenate(
                [y[q4:2 * q4, :], y[:q4, :], y[3 * q4:, :], y[2 * q4:3 * q4, :]], axis=0)
            blk = y * cos + partner * sin
        oat_ref[hh * HEAD_DIM:(hh + 1) * HEAD_DIM, :] = blk.astype(BF16)


def _inproj_a(x, g, wat, rope, qn, kn, *, tm=1024):
    s_len, d = x.shape
    assert s_len % tm == 0 and tm % GRID_W == 0 and wat.shape == (A_COLS, d)
    crow, srow, ccol, scol, expand = rope
    tile_rows = tm // GRID_W
    const = lambda i: (0, 0)
    return pl.pallas_call(
        _inproj_a_kernel,
        grid=(s_len // tm,),
        in_specs=[
            pl.BlockSpec((tm, d), lambda i: (i, 0)),
            pl.BlockSpec((1, d), const),
            pl.BlockSpec(wat.shape, const, pipeline_mode=pl.Buffered(1)),
            pl.BlockSpec((None, HEAD_DIM, 2 * tile_rows), lambda i: (i, 0, 0)),
            pl.BlockSpec((None, HEAD_DIM, 2 * tile_rows), lambda i: (i, 0, 0)),
            pl.BlockSpec((HEAD_DIM, tm), const),
            pl.BlockSpec((HEAD_DIM, tm), const),
            pl.BlockSpec((2 * tile_rows, tm), const),
            pl.BlockSpec((HEAD_DIM, HEAD_DIM), const),
            pl.BlockSpec((HEAD_DIM, HEAD_DIM), const),
        ],
        out_specs=[pl.BlockSpec((A_COLS, tm), lambda i: (0, i)),
                   pl.BlockSpec((tm, d), lambda i: (i, 0))],
        out_shape=[jax.ShapeDtypeStruct((A_COLS, s_len), BF16),
                   jax.ShapeDtypeStruct((s_len, d), BF16)],
        compiler_params=pltpu.CompilerParams(
            dimension_semantics=("parallel",), vmem_limit_bytes=VMEM_LIMIT),
        name="inproj_a",
    )(x, g, wat, crow, srow, ccol, scol, expand, qn, kn)


def _inproj_b_kernel(xn_ref, w_ref, o_ref):
    j = pl.program_id(1)

    @pl.when(j == 0)
    def _():
        nq = B_HEADS * HEAD_DIM
        acc = jnp.dot(xn_ref[...], w_ref[...], preferred_element_type=F32)
        o_ref[:, :nq] = (acc[:, :nq] * (SM_SCALE * LOG2E)).astype(BF16)
        o_ref[:, nq:] = acc[:, nq:].astype(BF16)

    @pl.when(j > 0)
    def _():
        o_ref[...] = jnp.dot(xn_ref[...], w_ref[...], preferred_element_type=F32).astype(BF16)


def _inproj_b(xn, w, *, tm=1024, tn=A_COLS):
    s_len, d = xn.shape
    n = w.shape[1]
    assert s_len % tm == 0 and n % tn == 0 and tn >= B_HEADS * HEAD_DIM
    return pl.pallas_call(
        _inproj_b_kernel,
        grid=(s_len // tm, n // tn),
        in_specs=[
            pl.BlockSpec((tm, d), lambda i, j: (i, 0)),
            pl.BlockSpec((d, tn), lambda i, j: (0, j)),
        ],
        out_specs=pl.BlockSpec((tm, tn), lambda i, j: (i, j)),
        out_shape=jax.ShapeDtypeStruct((s_len, n), BF16),
        compiler_params=pltpu.CompilerParams(
            dimension_semantics=("parallel", "arbitrary"), vmem_limit_bytes=VMEM_LIMIT),
        name="inproj_b",
    )(xn, w)


EXP2_SAFE_BOUND = 60.0


ONES_ROWS = 16


def _attn_a_kernel(*refs, tq, tk, nk, n_cast):
    q_ref, kt_ref, vt_ref = refs[:3]
    w32_refs = refs[3:3 + n_cast]
    o_ref = refs[3 + n_cast]
    w16_refs = refs[4 + n_cast:4 + 2 * n_cast]
    qt_ref, kn_ref, vxt_ref, kmax_ref, m_ref, acc_ref = refs[4 + 2 * n_cast:]
    i = pl.program_id(1)
    for w32, w16 in zip(w32_refs, w16_refs):
        w16[...] = w32[...].astype(BF16)

    @pl.when(i == 0)
    def _prepare_kv():
        row = lax.broadcasted_iota(jnp.int32, (ONES_ROWS, tk), 0)
        ones_rows = jnp.where(row == 0, 1.0, 0.0).astype(BF16)
        kmax = jnp.zeros((1, tk), F32)
        for c in range(nk):
            cols = slice(c * tk, (c + 1) * tk)
            ktf = kt_ref[:, cols].astype(F32)
            kn_ref[cols, :] = ktf.T.astype(BF16)
            vxt_ref[c, :HEAD_DIM, :] = vt_ref[:, cols]
            vxt_ref[c, HEAD_DIM:, :] = ones_rows
            kmax = jnp.maximum(kmax, jnp.sum(ktf * ktf, axis=0, keepdims=True))
        kmax_ref[0] = jnp.max(kmax)

    for h in range(A_GROUP):
        qt_ref[:, h * tq:(h + 1) * tq] = q_ref[h * HEAD_DIM:(h + 1) * HEAD_DIM, :]
    acc_ref[...] = jnp.zeros(acc_ref.shape, F32)
    qf = qt_ref[...].astype(F32)
    qmax = jnp.max(jnp.sum(qf * qf, axis=0, keepdims=True))
    no_max_needed = qmax * kmax_ref[0] <= EXP2_SAFE_BOUND * EXP2_SAFE_BOUND

    @pl.when(no_max_needed)
    def _plain():
        def body(j, carry):
            off = pl.multiple_of(j * tk, tk)
            s = jnp.dot(kn_ref[pl.ds(off, tk), :], qt_ref[...], preferred_element_type=F32)
            p = jnp.exp2(s).astype(BF16)
            acc_ref[...] += jnp.dot(vxt_ref[j], p, preferred_element_type=F32)
            return carry

        lax.fori_loop(0, nk, body, 0, unroll=16)

    @pl.when(jnp.logical_not(no_max_needed))
    def _online():
        m_ref[...] = jnp.full(m_ref.shape, -jnp.inf, F32)

        def body(j, carry):
            off = pl.multiple_of(j * tk, tk)
            s = jnp.dot(kn_ref[pl.ds(off, tk), :], qt_ref[...], preferred_element_type=F32)
            m_prev = m_ref[...]
            m_new = jnp.maximum(m_prev, jnp.max(s, axis=0, keepdims=True))
            p = jnp.exp2(s - m_new).astype(BF16)
            acc_ref[...] = jnp.exp2(m_prev - m_new) * acc_ref[...] + jnp.dot(
                vxt_ref[j], p, preferred_element_type=F32)
            m_ref[...] = m_new
            return carry

        lax.fori_loop(0, nk, body, 0)

    out_t = acc_ref[:HEAD_DIM, :] / acc_ref[HEAD_DIM:HEAD_DIM + 1, :]
    for h in range(A_GROUP):
        o_ref[:, h * HEAD_DIM:(h + 1) * HEAD_DIM] = out_t[:, h * tq:(h + 1) * tq].T.astype(BF16)


def _attn_a(proj_t, cast_ws, *, tq=512, tk=512):
    s_len = proj_t.shape[1]
    assert s_len % tq == 0 and s_len % tk == 0
    m_cols = A_GROUP * tq
    gw = A_GROUP * HEAD_DIM
    nk = s_len // tk
    n_i = s_len // tq
    n_steps = A_KV_HEADS * n_i
    bf16_rows = 16
    slabs = []
    for w in cast_ws:
        assert w.shape[0] % (n_steps * bf16_rows) == 0
        slabs.append(pl.BlockSpec((w.shape[0] // n_steps, w.shape[1]),
                                  lambda g, i: (g * n_i + i, 0)))
    kern = functools.partial(_attn_a_kernel, tq=tq, tk=tk, nk=nk, n_cast=len(cast_ws))
    outs = pl.pallas_call(
        kern,
        grid=(A_KV_HEADS, n_i),
        in_specs=[
            pl.BlockSpec((gw, tq), lambda g, i: (g, i)),
            pl.BlockSpec((HEAD_DIM, s_len), lambda g, i: (KA_COL + g, 0)),
            pl.BlockSpec((HEAD_DIM, s_len), lambda g, i: (VA_COL + g, 0)),
        ] + slabs,
        out_specs=[pl.BlockSpec((tq, gw), lambda g, i: (i, g))] + slabs,
        out_shape=[jax.ShapeDtypeStruct((s_len, A_Q_HEADS * HEAD_DIM), BF16)]
        + [jax.ShapeDtypeStruct(w.shape, BF16) for w in cast_ws],
        scratch_shapes=[
            pltpu.VMEM((HEAD_DIM, m_cols), BF16),
            pltpu.VMEM((s_len, HEAD_DIM), BF16),
            pltpu.VMEM((nk, HEAD_DIM + ONES_ROWS, tk), BF16),
            pltpu.SMEM((1,), F32),
            pltpu.VMEM((1, m_cols), F32),
            pltpu.VMEM((HEAD_DIM + ONES_ROWS, m_cols), F32),
        ],
        compiler_params=pltpu.CompilerParams(
            dimension_semantics=("arbitrary", "arbitrary"), vmem_limit_bytes=VMEM_LIMIT),
        name="attn_global",
    )(proj_t, proj_t, proj_t, *cast_ws)
    return outs[0], tuple(outs[1:])


def _nb_block_types(rows):
    last_q = rows - NB_QROWS
    return ((0, 0), (2 * NB_QROWS, 2 * NB_QROWS - WIN_ROWS // 2), (last_q, rows - NB_KROWS))


def _attn_b_kernel(rpb_ref, q_ref, k_ref, v_ref, o_ref, tcol_ref, tab_ref, vx_ref, *, rows, ub):
    h = pl.program_id(0)
    step = pl.program_id(1)
    nblk = rows // NB_QROWS
    n_dr = 2 * WIN_ROWS - 1
    n_dc = 2 * WIN_COLS - 1

    @pl.when(step == 0)
    def _build_tables():
        chunk = NB_TQ
        lane_v = lax.broadcasted_iota(jnp.int32, (chunk, HEAD_DIM), 1)
        ones_col = jnp.where(lane_v == 0, 1.0, 0.0).astype(BF16)

        def copy_v(c, carry):
            off_c = pl.multiple_of(c * chunk, chunk)
            vx_ref[pl.ds(off_c, chunk), :HEAD_DIM] = v_ref[pl.ds(off_c, chunk), :]
            vx_ref[pl.ds(off_c, chunk), HEAD_DIM:] = ones_col
            return carry

        lax.fori_loop(0, v_ref.shape[0] // chunk, copy_v, 0)

        c = lax.broadcasted_iota(jnp.int32, (GRID_W, 2 * GRID_W), 0)
        kc = lax.broadcasted_iota(jnp.int32, (GRID_W, 2 * GRID_W), 1) % GRID_W
        c0 = jnp.clip(c - WIN_COLS // 2, 0, GRID_W - WIN_COLS)
        col_ok = (kc >= c0) & (kc < c0 + WIN_COLS)
        dc = kc - c + (WIN_COLS - 1)
        base = h * (n_dr * n_dc)

        def row_body(a, carry):
            t = jnp.full((GRID_W, 2 * GRID_W), NEG_BIG, F32)
            for b in range(n_dc):
                t = jnp.where(dc == b, rpb_ref[base + a * n_dc + b] * LOG2E, t)
            tcol_ref[a] = jnp.where(col_ok, t, NEG_BIG)
            return carry

        lax.fori_loop(0, n_dr, row_body, 0)

        lane = lax.broadcasted_iota(jnp.int32, (GRID_W, 2 * GRID_W), 1)
        neg = jnp.full((GRID_W, 2 * GRID_W), NEG_BIG, F32)
        for t, (r_first, k_first) in enumerate(_nb_block_types(rows)):
            for qr in range(NB_QROWS):
                r = r_first + qr
                r0 = min(max(r - WIN_ROWS // 2, 0), rows - WIN_ROWS)
                for jj in range(NB_KROWS // 2):
                    halves = []
                    for kr in (2 * jj, 2 * jj + 1):
                        k_abs = k_first + kr
                        ok = r0 <= k_abs < r0 + WIN_ROWS
                        halves.append(tcol_ref[k_abs - r + WIN_ROWS - 1] if ok else neg)
                    tab_ref[t, qr * GRID_W:(qr + 1) * GRID_W,
                            jj * 2 * GRID_W:(jj + 1) * 2 * GRID_W] = jnp.where(
                                lane < GRID_W, halves[0], halves[1])

    for u in range(ub):
        pb = step * ub + u
        btype = jnp.where(pb == 0, 0, jnp.where(pb == nblk - 1, 2, 1))
        k_first = jnp.clip(pb * NB_QROWS - WIN_ROWS // 2, 0, rows - NB_KROWS)
        off = pl.multiple_of(k_first * GRID_W, GRID_W)
        ks = k_ref[pl.ds(off, NB_TK), :]
        s = lax.dot_general(q_ref[u * NB_TQ:(u + 1) * NB_TQ, :], ks, (((1,), (1,)), ((), ())),
                            preferred_element_type=F32)
        s = s + tab_ref[btype]
        m = jnp.max(s, axis=1, keepdims=True)
        p = jnp.exp2(s - m).astype(BF16)
        acc = jnp.dot(p, vx_ref[pl.ds(off, NB_TK), :], preferred_element_type=F32)
        o_ref[u * NB_TQ:(u + 1) * NB_TQ, :] = (
            acc[:, :HEAD_DIM] / acc[:, HEAD_DIM:HEAD_DIM + 1]).astype(BF16)


def _attn_b(proj, rpb_flat, *, max_ub=16):
    s_len = proj.shape[0]
    rows = s_len // GRID_W
    nblk = rows // NB_QROWS
    assert rows % NB_QROWS == 0 and rows >= 3 * NB_QROWS + WIN_ROWS
    ub = max(u for u in range(1, max_ub + 1) if nblk % u == 0)
    kern = functools.partial(_attn_b_kernel, rows=rows, ub=ub)
    return pl.pallas_call(
        kern,
        grid=(B_HEADS, nblk // ub),
        in_specs=[
            pl.BlockSpec(memory_space=pltpu.SMEM),
            pl.BlockSpec((ub * NB_TQ, HEAD_DIM), lambda h, p: (p, QB_COL + h)),
            pl.BlockSpec((s_len, HEAD_DIM), lambda h, p: (0, KB_COL + h)),
            pl.BlockSpec((s_len, HEAD_DIM), lambda h, p: (0, VB_COL + h)),
        ],
        out_specs=pl.BlockSpec((ub * NB_TQ, HEAD_DIM), lambda h, p: (p, h)),
        out_shape=jax.ShapeDtypeStruct((s_len, B_HEADS * HEAD_DIM), BF16),
        scratch_shapes=[
            pltpu.VMEM((2 * WIN_ROWS - 1, GRID_W, 2 * GRID_W), F32),
            pltpu.VMEM((3, NB_TQ, NB_TK), F32),
            pltpu.VMEM((s_len, 2 * HEAD_DIM), BF16),
        ],
        compiler_params=pltpu.CompilerParams(
            dimension_semantics=("parallel", "arbitrary"), vmem_limit_bytes=VMEM_LIMIT),
        name="attn_nbr",
    )(rpb_flat, proj, proj, proj)


ROW_SUB = 256


def _row_subblocks(tm):
    return [slice(r, r + ROW_SUB) for r in range(0, tm, ROW_SUB)]


def _oproj_kernel(oa_ref, ob_ref, wo_ref, x_ref, g_ref, h_ref):
    na = oa_ref.shape[1]
    for rows in _row_subblocks(x_ref.shape[0]):
        mix = jnp.dot(oa_ref[rows, :], wo_ref[:na, :], preferred_element_type=F32)
        mix = mix + jnp.dot(ob_ref[rows, :], wo_ref[na:, :], preferred_element_type=F32)
        h_ref[rows, :] = x_ref[rows, :] + _rms(mix, g_ref[...])


def _oproj(oa, ob, wo, x, g, *, tm=512):
    s_len, d = x.shape
    assert s_len % tm == 0 and tm % ROW_SUB == 0
    return pl.pallas_call(
        _oproj_kernel,
        grid=(s_len // tm,),
        in_specs=[
            pl.BlockSpec((tm, oa.shape[1]), lambda i: (i, 0)),
            pl.BlockSpec((tm, ob.shape[1]), lambda i: (i, 0)),
            pl.BlockSpec(wo.shape, lambda i: (0, 0), pipeline_mode=pl.Buffered(1)),
            pl.BlockSpec((tm, d), lambda i: (i, 0)),
            pl.BlockSpec((1, d), lambda i: (0, 0)),
        ],
        out_specs=pl.BlockSpec((tm, d), lambda i: (i, 0)),
        out_shape=jax.ShapeDtypeStruct((s_len, d), F32),
        compiler_params=pltpu.CompilerParams(
            dimension_semantics=("parallel",), vmem_limit_bytes=VMEM_LIMIT),
        name="oproj",
    )(oa, ob, wo, x, g)


def _mlp_kernel(h_ref, gpre_ref, wup_ref, wdn_ref, gpost_ref, o_ref, xn_ref, acc_ref):
    f = pl.program_id(1)

    def ff_chunk():
        u = jnp.dot(xn_ref[...], wup_ref[...], preferred_element_type=F32)
        a = jnp.square(jnp.maximum(u, 0.0)).astype(BF16)
        return jnp.dot(a, wdn_ref[...], preferred_element_type=F32)

    @pl.when(f == 0)
    def _():
        xn_ref[...] = _rms(h_ref[...], gpre_ref[...]).astype(BF16)
        acc_ref[...] = ff_chunk()

    last = pl.num_programs(1) - 1

    @pl.when((f > 0) & (f < last))
    def _():
        acc_ref[...] += ff_chunk()

    @pl.when(f == last)
    def _():
        o_ref[...] = h_ref[...] + _rms(acc_ref[...] + ff_chunk(), gpost_ref[...])


def _mlp(h, gpre, wup, wdn, gpost, *, tm=512, tf=1024):
    s_len, d = h.shape
    d_ff = wup.shape[1]
    assert s_len % tm == 0 and d_ff % tf == 0
    return pl.pallas_call(
        _mlp_kernel,
        grid=(s_len // tm, d_ff // tf),
        in_specs=[
            pl.BlockSpec((tm, d), lambda i, f: (i, 0)),
            pl.BlockSpec((1, d), lambda i, f: (0, 0)),
            pl.BlockSpec((d, tf), lambda i, f: (0, f)),
            pl.BlockSpec((tf, d), lambda i, f: (f, 0)),
            pl.BlockSpec((1, d), lambda i, f: (0, 0)),
        ],
        out_specs=pl.BlockSpec((tm, d), lambda i, f: (i, 0)),
        out_shape=jax.ShapeDtypeStruct((s_len, d), F32),
        scratch_shapes=[pltpu.VMEM((tm, d), BF16), pltpu.VMEM((tm, d), F32)],
        compiler_params=pltpu.CompilerParams(
            dimension_semantics=("parallel", "arbitrary"), vmem_limit_bytes=VMEM_LIMIT),
        name="mlp",
    )(h, gpre, wup, wdn, gpost)


def _ple_kernel(h_ref, p_ref, gpre_ref, wg_ref, wp_ref, gpost_ref, o_ref):
    for rows in _row_subblocks(h_ref.shape[0]):
        h = h_ref[rows, :]
        xn = _rms(h, gpre_ref[...]).astype(BF16)
        gate = jax.nn.sigmoid(jnp.dot(xn, wg_ref[...], preferred_element_type=F32))
        e = jnp.dot(p_ref[rows, :].astype(BF16), wp_ref[...], preferred_element_type=F32) * gate
        o_ref[rows, :] = h + _rms(e, gpost_ref[...])


def _ple(h, p, gpre, wg, wp, gpost, *, tm=512):
    s_len, d = h.shape
    dp = p.shape[1]
    assert s_len % tm == 0 and tm % ROW_SUB == 0
    return pl.pallas_call(
        _ple_kernel,
        grid=(s_len // tm,),
        in_specs=[
            pl.BlockSpec((tm, d), lambda i: (i, 0)),
            pl.BlockSpec((tm, dp), lambda i: (i, 0)),
            pl.BlockSpec((1, d), lambda i: (0, 0)),
            pl.BlockSpec(wg.shape, lambda i: (0, 0), pipeline_mode=pl.Buffered(1)),
            pl.BlockSpec(wp.shape, lambda i: (0, 0), pipeline_mode=pl.Buffered(1)),
            pl.BlockSpec((1, d), lambda i: (0, 0)),
        ],
        out_specs=pl.BlockSpec((tm, d), lambda i: (i, 0)),
        out_shape=jax.ShapeDtypeStruct((s_len, d), F32),
        compiler_params=pltpu.CompilerParams(
            dimension_semantics=("parallel",), vmem_limit_bytes=VMEM_LIMIT),
        name="ple",
    )(h, p, gpre, wg, wp, gpost)


def _rope_tables(s_len, tm):
    rows = s_len // GRID_W
    tile_rows = tm // GRID_W
    n_freq = HEAD_DIM // 4
    freqs = ROPE_THETA ** (-jnp.arange(n_freq, dtype=F32) / n_freq)
    ar = freqs[:, None] * jnp.arange(rows, dtype=F32)[None, :]
    ac = freqs[:, None] * jnp.arange(GRID_W, dtype=F32)[None, :]
    zr, zc = jnp.zeros_like(ar), jnp.zeros_like(ac)

    def row_table(t):
        hi = t.astype(BF16)
        lo = (t - hi.astype(F32)).astype(BF16)
        per_tile = lambda a: a.reshape(HEAD_DIM, rows // tile_rows, tile_rows).swapaxes(0, 1)
        return jnp.concatenate([per_tile(hi), per_tile(lo)], axis=-1)

    crow = row_table(jnp.concatenate([jnp.cos(ar), jnp.cos(ar), zr, zr], axis=0))
    srow = row_table(jnp.concatenate([-jnp.sin(ar), jnp.sin(ar), zr, zr], axis=0))
    ccol = jnp.tile(jnp.concatenate([zc, zc, jnp.cos(ac), jnp.cos(ac)], axis=0), (1, tile_rows))
    scol = jnp.tile(jnp.concatenate([zc, zc, -jnp.sin(ac), jnp.sin(ac)], axis=0), (1, tile_rows))
    tok_row = jnp.arange(tm, dtype=jnp.int32)[None, :] // GRID_W
    line = jnp.arange(2 * tile_rows, dtype=jnp.int32)[:, None] % tile_rows
    expand = (tok_row == line).astype(BF16)
    return crow, srow, ccol, scol, expand


def kernel(x, p, pre_mix_norm, w_in, q_norm, k_norm, rel_pos_bias, w_o, post_mix_norm,
           pre_mlp_norm, w_up, w_down, post_mlp_norm, pre_ple_norm, w_ple_gate, w_ple_proj,
           post_ple_norm):
    b, s_len, d = x.shape
    depth = w_in.shape[0]
    tm_in = 1024
    rope = _rope_tables(s_len, tm_in)
    outs = []
    for bi in range(b):
        h = x[bi]
        for i in range(depth):
            lanes = (HEAD_DIM, HEAD_DIM)
            qn = jnp.broadcast_to((q_norm[i] * (SM_SCALE * LOG2E))[:, None], lanes)
            kn = jnp.broadcast_to(k_norm[i][:, None], lanes)
            wat = w_in[i][:, :A_COLS].T.astype(BF16)
            proj_t, xn = _inproj_a(h, pre_mix_norm[i].reshape(1, d), wat, rope, qn, kn, tm=tm_in)
            proj = _inproj_b(xn, w_in[i][:, A_COLS:].astype(BF16))
            out_a, (wo16, wup16, wdn16, wg16) = _attn_a(
                proj_t, (w_o[i], w_up[i], w_down[i], w_ple_gate[i]))
            out_b = _attn_b(proj, rel_pos_bias[i].reshape(-1))
            h = _oproj(out_a, out_b, wo16, h, post_mix_norm[i].reshape(1, d))
            h = _mlp(h, pre_mlp_norm[i].reshape(1, d), wup16, wdn16, post_mlp_norm[i].reshape(1, d))
            h = _ple(h, p[i, bi], pre_ple_norm[i].reshape(1, d), wg16,
                     w_ple_proj[i].astype(BF16), post_ple_norm[i].reshape(1, d))
        outs.append(h)
    return jnp.stack(outs, axis=0)
```

```python
import functools
import math

import jax
import jax.numpy as jnp
from jax import lax
from jax.experimental import pallas as pl
from jax.experimental.pallas import tpu as pltpu

F32 = jnp.float32
BF16 = jnp.bfloat16

HEAD_DIM = 128
GRID_W = 64
A_Q_HEADS = 8
A_KV_HEADS = 2
A_GROUP = A_Q_HEADS // A_KV_HEADS
B_HEADS = 8
WIN_ROWS = 8
WIN_COLS = 16
ROPE_THETA = 10000.0
NORM_EPS = 1e-6
LOG2E = math.log2(math.e)
SM_SCALE = 1.0 / math.sqrt(HEAD_DIM)
NEG_BIG = -1e30

QA_COL, KA_COL, VA_COL = 0, 8, 10
A_COLS = 12 * HEAD_DIM
QB_COL, KB_COL, VB_COL = 0, 8, 16

V7X_VMEM_BYTES = 64 * 1024 * 1024
VMEM_LIMIT = V7X_VMEM_BYTES - 8 * 1024 * 1024

NB_QROWS = 4
NB_KROWS = NB_QROWS + WIN_ROWS
NB_TQ = NB_QROWS * GRID_W
NB_TK = NB_KROWS * GRID_W


def _rms(x, g):
    return x * lax.rsqrt(jnp.mean(x * x, axis=-1, keepdims=True) + NORM_EPS) * g


def _inproj_a_kernel(x_ref, g_ref, wat_ref, crow_ref, srow_ref, ccol_ref, scol_ref, expand_ref,
                     qn_ref, kn_ref, oat_ref, xn_ref):
    tm = x_ref.shape[0]
    xn = _rms(x_ref[...], g_ref[...]).astype(BF16)
    xn_ref[...] = xn
    acc = lax.dot_general(wat_ref[...], xn, (((1,), (1,)), ((), ())),
                          preferred_element_type=F32)

    def per_token(row_ref, col_ref):
        return jnp.dot(row_ref[...], expand_ref[...], preferred_element_type=F32) + col_ref[...]

    cos = per_token(crow_ref, ccol_ref)
    sin = per_token(srow_ref, scol_ref)
    q4 = HEAD_DIM // 4
    reps = tm // HEAD_DIM
    for hh in range(A_COLS // HEAD_DIM):
        blk = acc[hh * HEAD_DIM:(hh + 1) * HEAD_DIM, :]
        if hh < VA_COL:
            gain = jnp.tile(qn_ref[...] if hh < KA_COL else kn_ref[...], (1, reps))
            ssq = jnp.sum(blk * blk, axis=0, keepdims=True)
            y = blk * lax.rsqrt(ssq * (1.0 / HEAD_DIM) + NORM_EPS) * gain
            partner = jnp.concatenate(
                [y[q4:2 * q4, :], y[:q4, :], y[3 * q4:, :], y[2 * q4:3 * q4, :]], axis=0)
            blk = y * cos + partner * sin
        oat_ref[hh * HEAD_DIM:(hh + 1) * HEAD_DIM, :] = blk.astype(BF16)


def _inproj_a(x, g, wat, rope, qn, kn, *, tm=1024):
    s_len, d = x.shape
    assert s_len % tm == 0 and tm % GRID_W == 0 and wat.shape == (A_COLS, d)
    crow, srow, ccol, scol, expand = rope
    tile_rows = tm // GRID_W
    const = lambda i: (0, 0)
    return pl.pallas_call(
        _inproj_a_kernel,
        grid=(s_len // tm,),
        in_specs=[
            pl.BlockSpec((tm, d), lambda i: (i, 0)),
            pl.BlockSpec((1, d), const),
            pl.BlockSpec(wat.shape, const, pipeline_mode=pl.Buffered(1)),
            pl.BlockSpec((None, HEAD_DIM, 2 * tile_rows), lambda i: (i, 0, 0)),
            pl.BlockSpec((None, HEAD_DIM, 2 * tile_rows), lambda i: (i, 0, 0)),
            pl.BlockSpec((HEAD_DIM, tm), const),
            pl.BlockSpec((HEAD_DIM, tm), const),
            pl.BlockSpec((2 * tile_rows, tm), const),
            pl.BlockSpec((HEAD_DIM, HEAD_DIM), const),
            pl.BlockSpec((HEAD_DIM, HEAD_DIM), const),
        ],
        out_specs=[pl.BlockSpec((A_COLS, tm), lambda i: (0, i)),
                   pl.BlockSpec((tm, d), lambda i: (i, 0))],
        out_shape=[jax.ShapeDtypeStruct((A_COLS, s_len), BF16),
                   jax.ShapeDtypeStruct((s_len, d), BF16)],
        compiler_params=pltpu.CompilerParams(
            dimension_semantics=("parallel",), vmem_limit_bytes=VMEM_LIMIT),
        name="inproj_a",
    )(x, g, wat, crow, srow, ccol, scol, expand, qn, kn)


def _inproj_b_kernel(xn_ref, w_ref, o_ref):
    j = pl.program_id(1)

    @pl.when(j == 0)
    def _():
        nq = B_HEADS * HEAD_DIM
        acc = jnp.dot(xn_ref[...], w_ref[...], preferred_element_type=F32)
        o_ref[:, :nq] = (acc[:, :nq] * (SM_SCALE * LOG2E)).astype(BF16)
        o_ref[:, nq:] = acc[:, nq:].astype(BF16)

    @pl.when(j > 0)
    def _():
        o_ref[...] = jnp.dot(xn_ref[...], w_ref[...], preferred_element_type=F32).astype(BF16)


def _inproj_b(xn, w, *, tm=1024, tn=A_COLS):
    s_len, d = xn.shape
    n = w.shape[1] - A_COLS
    assert s_len % tm == 0 and n % tn == 0 and A_COLS % tn == 0 and tn >= B_HEADS * HEAD_DIM
    skip = A_COLS // tn
    return pl.pallas_call(
        _inproj_b_kernel,
        grid=(s_len // tm, n // tn),
        in_specs=[
            pl.BlockSpec((tm, d), lambda i, j: (i, 0)),
            pl.BlockSpec((d, tn), lambda i, j: (0, j + skip)),
        ],
        out_specs=pl.BlockSpec((tm, tn), lambda i, j: (i, j)),
        out_shape=jax.ShapeDtypeStruct((s_len, n), BF16),
        compiler_params=pltpu.CompilerParams(
            dimension_semantics=("parallel", "arbitrary"), vmem_limit_bytes=VMEM_LIMIT),
        name="inproj_b",
    )(xn, w)


EXP2_SAFE_BOUND = 60.0


ONES_ROWS = 16


def _attn_a_kernel(*refs, tq, tk, nk, n_cast):
    q_ref, kt_ref, vt_ref = refs[:3]
    w32_refs = refs[3:3 + n_cast]
    o_ref = refs[3 + n_cast]
    w16_refs = refs[4 + n_cast:4 + 2 * n_cast]
    qt_ref, kn_ref, vxt_ref, kmax_ref, m_ref, acc_ref = refs[4 + 2 * n_cast:]
    i = pl.program_id(1)
    for w32, w16 in zip(w32_refs, w16_refs):
        w16[...] = w32[...].astype(BF16)

    @pl.when(i == 0)
    def _prepare_kv():
        row = lax.broadcasted_iota(jnp.int32, (ONES_ROWS, tk), 0)
        ones_rows = jnp.where(row == 0, 1.0, 0.0).astype(BF16)
        kmax = jnp.zeros((1, tk), F32)
        for c in range(nk):
            cols = slice(c * tk, (c + 1) * tk)
            ktf = kt_ref[:, cols].astype(F32)
            kn_ref[cols, :] = ktf.T.astype(BF16)
            vxt_ref[c, :HEAD_DIM, :] = vt_ref[:, cols]
            vxt_ref[c, HEAD_DIM:, :] = ones_rows
            kmax = jnp.maximum(kmax, jnp.sum(ktf * ktf, axis=0, keepdims=True))
        kmax_ref[0] = jnp.max(kmax)

    for h in range(A_GROUP):
        qt_ref[:, h * tq:(h + 1) * tq] = q_ref[h * HEAD_DIM:(h + 1) * HEAD_DIM, :]
    acc_ref[...] = jnp.zeros(acc_ref.shape, F32)
    qf = qt_ref[...].astype(F32)
    qmax = jnp.max(jnp.sum(qf * qf, axis=0, keepdims=True))
    no_max_needed = qmax * kmax_ref[0] <= EXP2_SAFE_BOUND * EXP2_SAFE_BOUND

    @pl.when(no_max_needed)
    def _plain():
        def body(j, carry):
            off = pl.multiple_of(j * tk, tk)
            s = jnp.dot(kn_ref[pl.ds(off, tk), :], qt_ref[...], preferred_element_type=F32)
            p = jnp.exp2(s).astype(BF16)
            acc_ref[...] += jnp.dot(vxt_ref[j], p, preferred_element_type=F32)
            return carry

        lax.fori_loop(0, nk, body, 0, unroll=16)

    @pl.when(jnp.logical_not(no_max_needed))
    def _online():
        m_ref[...] = jnp.full(m_ref.shape, -jnp.inf, F32)

        def body(j, carry):
            off = pl.multiple_of(j * tk, tk)
            s = jnp.dot(kn_ref[pl.ds(off, tk), :], qt_ref[...], preferred_element_type=F32)
            m_prev = m_ref[...]
            m_new = jnp.maximum(m_prev, jnp.max(s, axis=0, keepdims=True))
            p = jnp.exp2(s - m_new).astype(BF16)
            acc_ref[...] = jnp.exp2(m_prev - m_new) * acc_ref[...] + jnp.dot(
                vxt_ref[j], p, preferred_element_type=F32)
            m_ref[...] = m_new
            return carry

        lax.fori_loop(0, nk, body, 0)

    out_t = acc_ref[:HEAD_DIM, :] / acc_ref[HEAD_DIM:HEAD_DIM + 1, :]
    for h in range(A_GROUP):
        o_ref[:, h * HEAD_DIM:(h + 1) * HEAD_DIM] = out_t[:, h * tq:(h + 1) * tq].T.astype(BF16)


def _attn_a(proj_t, cast_ws, *, tq=512, tk=512):
    s_len = proj_t.shape[1]
    assert s_len % tq == 0 and s_len % tk == 0
    m_cols = A_GROUP * tq
    gw = A_GROUP * HEAD_DIM
    nk = s_len // tk
    n_i = s_len // tq
    n_steps = A_KV_HEADS * n_i
    bf16_rows = 16
    slabs = []
    for w in cast_ws:
        assert w.shape[0] % (n_steps * bf16_rows) == 0
        slabs.append(pl.BlockSpec((w.shape[0] // n_steps, w.shape[1]),
                                  lambda g, i: (g * n_i + i, 0)))
    kern = functools.partial(_attn_a_kernel, tq=tq, tk=tk, nk=nk, n_cast=len(cast_ws))
    outs = pl.pallas_call(
        kern,
        grid=(A_KV_HEADS, n_i),
        in_specs=[
            pl.BlockSpec((gw, tq), lambda g, i: (g, i)),
            pl.BlockSpec((HEAD_DIM, s_len), lambda g, i: (KA_COL + g, 0)),
            pl.BlockSpec((HEAD_DIM, s_len), lambda g, i: (VA_COL + g, 0)),
        ] + slabs,
        out_specs=[pl.BlockSpec((tq, gw), lambda g, i: (i, g))] + slabs,
        out_shape=[jax.ShapeDtypeStruct((s_len, A_Q_HEADS * HEAD_DIM), BF16)]
        + [jax.ShapeDtypeStruct(w.shape, BF16) for w in cast_ws],
        scratch_shapes=[
            pltpu.VMEM((HEAD_DIM, m_cols), BF16),
            pltpu.VMEM((s_len, HEAD_DIM), BF16),
            pltpu.VMEM((nk, HEAD_DIM + ONES_ROWS, tk), BF16),
            pltpu.SMEM((1,), F32),
            pltpu.VMEM((1, m_cols), F32),
            pltpu.VMEM((HEAD_DIM + ONES_ROWS, m_cols), F32),
        ],
        compiler_params=pltpu.CompilerParams(
            dimension_semantics=("arbitrary", "arbitrary"), vmem_limit_bytes=VMEM_LIMIT),
        name="attn_global",
    )(proj_t, proj_t, proj_t, *cast_ws)
    return outs[0], tuple(outs[1:])


def _nb_block_types(rows):
    last_q = rows - NB_QROWS
    return ((0, 0), (2 * NB_QROWS, 2 * NB_QROWS - WIN_ROWS // 2), (last_q, rows - NB_KROWS))


def _attn_b_kernel(rpb_ref, q_ref, k_ref, v_ref, o_ref, tcol_ref, tab_ref, vx_ref, *, rows, ub):
    h = pl.program_id(0)
    step = pl.program_id(1)
    nblk = rows // NB_QROWS
    n_dr = 2 * WIN_ROWS - 1
    n_dc = 2 * WIN_COLS - 1

    @pl.when(step == 0)
    def _build_tables():
        chunk = NB_TQ
        lane_v = lax.broadcasted_iota(jnp.int32, (chunk, HEAD_DIM), 1)
        ones_col = jnp.where(lane_v == 0, 1.0, 0.0).astype(BF16)

        def copy_v(c, carry):
            off_c = pl.multiple_of(c * chunk, chunk)
            vx_ref[pl.ds(off_c, chunk), :HEAD_DIM] = v_ref[pl.ds(off_c, chunk), :]
            vx_ref[pl.ds(off_c, chunk), HEAD_DIM:] = ones_col
            return carry

        lax.fori_loop(0, v_ref.shape[0] // chunk, copy_v, 0)

        c = lax.broadcasted_iota(jnp.int32, (GRID_W, 2 * GRID_W), 0)
        kc = lax.broadcasted_iota(jnp.int32, (GRID_W, 2 * GRID_W), 1) % GRID_W
        c0 = jnp.clip(c - WIN_COLS // 2, 0, GRID_W - WIN_COLS)
        col_ok = (kc >= c0) & (kc < c0 + WIN_COLS)
        dc = kc - c + (WIN_COLS - 1)
        base = h * (n_dr * n_dc)

        def row_body(a, carry):
            t = jnp.full((GRID_W, 2 * GRID_W), NEG_BIG, F32)
            for b in range(n_dc):
                t = jnp.where(dc == b, rpb_ref[base + a * n_dc + b] * LOG2E, t)
            tcol_ref[a] = jnp.where(col_ok, t, NEG_BIG)
            return carry

        lax.fori_loop(0, n_dr, row_body, 0)

        lane = lax.broadcasted_iota(jnp.int32, (GRID_W, 2 * GRID_W), 1)
        neg = jnp.full((GRID_W, 2 * GRID_W), NEG_BIG, F32)
        for t, (r_first, k_first) in enumerate(_nb_block_types(rows)):
            for qr in range(NB_QROWS):
                r = r_first + qr
                r0 = min(max(r - WIN_ROWS // 2, 0), rows - WIN_ROWS)
                for jj in range(NB_KROWS // 2):
                    halves = []
                    for kr in (2 * jj, 2 * jj + 1):
                        k_abs = k_first + kr
                        ok = r0 <= k_abs < r0 + WIN_ROWS
                        halves.append(tcol_ref[k_abs - r + WIN_ROWS - 1] if ok else neg)
                    tab_ref[t, qr * GRID_W:(qr + 1) * GRID_W,
                            jj * 2 * GRID_W:(jj + 1) * 2 * GRID_W] = jnp.where(
                                lane < GRID_W, halves[0], halves[1])

    for u in range(ub):
        pb = step * ub + u
        btype = jnp.where(pb == 0, 0, jnp.where(pb == nblk - 1, 2, 1))
        k_first = jnp.clip(pb * NB_QROWS - WIN_ROWS // 2, 0, rows - NB_KROWS)
        off = pl.multiple_of(k_first * GRID_W, GRID_W)
        ks = k_ref[pl.ds(off, NB_TK), :]
        s = lax.dot_general(q_ref[u * NB_TQ:(u + 1) * NB_TQ, :], ks, (((1,), (1,)), ((), ())),
                            preferred_element_type=F32)
        s = s + tab_ref[btype]
        m = jnp.max(s, axis=1, keepdims=True)
        p = jnp.exp2(s - m).astype(BF16)
        acc = jnp.dot(p, vx_ref[pl.ds(off, NB_TK), :], preferred_element_type=F32)
        o_ref[u * NB_TQ:(u + 1) * NB_TQ, :] = (
            acc[:, :HEAD_DIM] / acc[:, HEAD_DIM:HEAD_DIM + 1]).astype(BF16)


def _attn_b(proj, rpb_flat, *, max_ub=32):
    s_len = proj.shape[0]
    rows = s_len // GRID_W
    nblk = rows // NB_QROWS
    assert rows % NB_QROWS == 0 and rows >= 3 * NB_QROWS + WIN_ROWS
    ub = max(u for u in range(1, max_ub + 1) if nblk % u == 0)
    kern = functools.partial(_attn_b_kernel, rows=rows, ub=ub)
    return pl.pallas_call(
        kern,
        grid=(B_HEADS, nblk // ub),
        in_specs=[
            pl.BlockSpec(memory_space=pltpu.SMEM),
            pl.BlockSpec((ub * NB_TQ, HEAD_DIM), lambda h, p: (p, QB_COL + h)),
            pl.BlockSpec((s_len, HEAD_DIM), lambda h, p: (0, KB_COL + h)),
            pl.BlockSpec((s_len, HEAD_DIM), lambda h, p: (0, VB_COL + h)),
        ],
        out_specs=pl.BlockSpec((ub * NB_TQ, HEAD_DIM), lambda h, p: (p, h)),
        out_shape=jax.ShapeDtypeStruct((s_len, B_HEADS * HEAD_DIM), BF16),
        scratch_shapes=[
            pltpu.VMEM((2 * WIN_ROWS - 1, GRID_W, 2 * GRID_W), F32),
            pltpu.VMEM((3, NB_TQ, NB_TK), F32),
            pltpu.VMEM((s_len, 2 * HEAD_DIM), BF16),
        ],
        compiler_params=pltpu.CompilerParams(
            dimension_semantics=("parallel", "arbitrary"), vmem_limit_bytes=VMEM_LIMIT),
        name="attn_nbr",
    )(rpb_flat, proj, proj, proj)


ROW_SUB = 256


def _row_subblocks(tm):
    return [slice(r, r + ROW_SUB) for r in range(0, tm, ROW_SUB)]


def _oproj_kernel(oa_ref, ob_ref, wo_ref, x_ref, g_ref, h_ref):
    na = oa_ref.shape[1]
    for rows in _row_subblocks(x_ref.shape[0]):
        mix = jnp.dot(oa_ref[rows, :], wo_ref[:na, :], preferred_element_type=F32)
        mix = mix + jnp.dot(ob_ref[rows, :], wo_ref[na:, :], preferred_element_type=F32)
        h_ref[rows, :] = x_ref[rows, :] + _rms(mix, g_ref[...])


def _oproj(oa, ob, wo, x, g, *, tm=512):
    s_len, d = x.shape
    assert s_len % tm == 0 and tm % ROW_SUB == 0
    return pl.pallas_call(
        _oproj_kernel,
        grid=(s_len // tm,),
        in_specs=[
            pl.BlockSpec((tm, oa.shape[1]), lambda i: (i, 0)),
            pl.BlockSpec((tm, ob.shape[1]), lambda i: (i, 0)),
            pl.BlockSpec(wo.shape, lambda i: (0, 0), pipeline_mode=pl.Buffered(1)),
            pl.BlockSpec((tm, d), lambda i: (i, 0)),
            pl.BlockSpec((1, d), lambda i: (0, 0)),
        ],
        out_specs=pl.BlockSpec((tm, d), lambda i: (i, 0)),
        out_shape=jax.ShapeDtypeStruct((s_len, d), F32),
        compiler_params=pltpu.CompilerParams(
            dimension_semantics=("parallel",), vmem_limit_bytes=VMEM_LIMIT),
        name="oproj",
    )(oa, ob, wo, x, g)


def _mlp_kernel(h_ref, gpre_ref, wup_hbm, wdn_hbm, gpost_ref, o_ref, wup_buf, wdn_buf, sem, xn_ref,
                acc_ref, *, tf, n_ff):
    i = pl.program_id(0)

    def copies(f, slot):
        cols = pl.ds(f * tf, tf)
        return (pltpu.make_async_copy(wup_hbm.at[:, cols], wup_buf.at[slot], sem.at[0, slot]),
                pltpu.make_async_copy(wdn_hbm.at[cols, :], wdn_buf.at[slot], sem.at[1, slot]))

    def start(f, slot):
        for c in copies(f, slot):
            c.start()

    @pl.when(i == 0)
    def _():
        start(0, 0)

    xn_ref[...] = _rms(h_ref[...], gpre_ref[...]).astype(BF16)
    for f in range(n_ff):
        slot = f % 2
        for c in copies(f, slot):
            c.wait()
        if f + 1 < n_ff:
            start(f + 1, 1 - slot)
        else:
            assert n_ff % 2 == 0

            @pl.when(i + 1 < pl.num_programs(0))
            def _():
                start(0, 0)

        u = jnp.dot(xn_ref[...], wup_buf[slot], preferred_element_type=F32)
        a = jnp.square(jnp.maximum(u, 0.0)).astype(BF16)
        part = jnp.dot(a, wdn_buf[slot], preferred_element_type=F32)
        if f == 0:
            acc_ref[...] = part
        elif f + 1 < n_ff:
            acc_ref[...] += part
        else:
            o_ref[...] = h_ref[...] + _rms(acc_ref[...] + part, gpost_ref[...])


def _mlp(h, gpre, wup, wdn, gpost, *, tm=512, tf=1024):
    s_len, d = h.shape
    d_ff = wup.shape[1]
    assert s_len % tm == 0 and d_ff % tf == 0
    kern = functools.partial(_mlp_kernel, tf=tf, n_ff=d_ff // tf)
    return pl.pallas_call(
        kern,
        grid=(s_len // tm,),
        in_specs=[
            pl.BlockSpec((tm, d), lambda i: (i, 0)),
            pl.BlockSpec((1, d), lambda i: (0, 0)),
            pl.BlockSpec(memory_space=pl.ANY),
            pl.BlockSpec(memory_space=pl.ANY),
            pl.BlockSpec((1, d), lambda i: (0, 0)),
        ],
        out_specs=pl.BlockSpec((tm, d), lambda i: (i, 0)),
        out_shape=jax.ShapeDtypeStruct((s_len, d), F32),
        scratch_shapes=[pltpu.VMEM((2, d, tf), BF16), pltpu.VMEM((2, tf, d), BF16),
                        pltpu.SemaphoreType.DMA((2, 2)), pltpu.VMEM((tm, d), BF16),
                        pltpu.VMEM((tm, d), F32)],
        compiler_params=pltpu.CompilerParams(
            dimension_semantics=("arbitrary",), vmem_limit_bytes=VMEM_LIMIT),
        name="mlp",
    )(h, gpre, wup, wdn, gpost)


def _ple_kernel(h_ref, p_ref, gpre_ref, wg_ref, wp_ref, gpost_ref, o_ref):
    for rows in _row_subblocks(h_ref.shape[0]):
        h = h_ref[rows, :]
        xn = _rms(h, gpre_ref[...]).astype(BF16)
        gate = jax.nn.sigmoid(jnp.dot(xn, wg_ref[...], preferred_element_type=F32))
        e = jnp.dot(p_ref[rows, :].astype(BF16), wp_ref[...], preferred_element_type=F32) * gate
        o_ref[rows, :] = h + _rms(e, gpost_ref[...])


def _ple(h, p, gpre, wg, wp, gpost, *, tm=512):
    s_len, d = h.shape
    dp = p.shape[1]
    assert s_len % tm == 0 and tm % ROW_SUB == 0
    return pl.pallas_call(
        _ple_kernel,
        grid=(s_len // tm,),
        in_specs=[
            pl.BlockSpec((tm, d), lambda i: (i, 0)),
            pl.BlockSpec((tm, dp), lambda i: (i, 0)),
            pl.BlockSpec((1, d), lambda i: (0, 0)),
            pl.BlockSpec(wg.shape, lambda i: (0, 0), pipeline_mode=pl.Buffered(1)),
            pl.BlockSpec(wp.shape, lambda i: (0, 0), pipeline_mode=pl.Buffered(1)),
            pl.BlockSpec((1, d), lambda i: (0, 0)),
        ],
        out_specs=pl.BlockSpec((tm, d), lambda i: (i, 0)),
        out_shape=jax.ShapeDtypeStruct((s_len, d), F32),
        compiler_params=pltpu.CompilerParams(
            dimension_semantics=("parallel",), vmem_limit_bytes=VMEM_LIMIT),
        name="ple",
    )(h, p, gpre, wg, wp, gpost)


def _rope_tables(s_len, tm):
    rows = s_len // GRID_W
    tile_rows = tm // GRID_W
    n_freq = HEAD_DIM // 4
    freqs = ROPE_THETA ** (-jnp.arange(n_freq, dtype=F32) / n_freq)
    ar = freqs[:, None] * jnp.arange(rows, dtype=F32)[None, :]
    ac = freqs[:, None] * jnp.arange(GRID_W, dtype=F32)[None, :]
    zr, zc = jnp.zeros_like(ar), jnp.zeros_like(ac)

    def row_table(t):
        hi = t.astype(BF16)
        lo = (t - hi.astype(F32)).astype(BF16)
        per_tile = lambda a: a.reshape(HEAD_DIM, rows // tile_rows, tile_rows).swapaxes(0, 1)
        return jnp.concatenate([per_tile(hi), per_tile(lo)], axis=-1)

    crow = row_table(jnp.concatenate([jnp.cos(ar), jnp.cos(ar), zr, zr], axis=0))
    srow = row_table(jnp.concatenate([-jnp.sin(ar), jnp.sin(ar), zr, zr], axis=0))
    ccol = jnp.tile(jnp.concatenate([zc, zc, jnp.cos(ac), jnp.cos(ac)], axis=0), (1, tile_rows))
    scol = jnp.tile(jnp.concatenate([zc, zc, -jnp.sin(ac), jnp.sin(ac)], axis=0), (1, tile_rows))
    tok_row = jnp.arange(tm, dtype=jnp.int32)[None, :] // GRID_W
    line = jnp.arange(2 * tile_rows, dtype=jnp.int32)[:, None] % tile_rows
    expand = (tok_row == line).astype(BF16)
    return crow, srow, ccol, scol, expand


def kernel(x, p, pre_mix_norm, w_in, q_norm, k_norm, rel_pos_bias, w_o, post_mix_norm,
           pre_mlp_norm, w_up, w_down, post_mlp_norm, pre_ple_norm, w_ple_gate, w_ple_proj,
           post_ple_norm):
    b, s_len, d = x.shape
    depth = w_in.shape[0]
    tm_in = 1024
    rope = _rope_tables(s_len, tm_in)
    outs = []
    for bi in range(b):
        h = x[bi]
        for i in range(depth):
            lanes = (HEAD_DIM, HEAD_DIM)
            qn = jnp.broadcast_to((q_norm[i] * (SM_SCALE * LOG2E))[:, None], lanes)
            kn = jnp.broadcast_to(k_norm[i][:, None], lanes)
            w16 = w_in[i].astype(BF16)
            wat = w16[:, :A_COLS].T
            proj_t, xn = _inproj_a(h, pre_mix_norm[i].reshape(1, d), wat, rope, qn, kn, tm=tm_in)
            proj = _inproj_b(xn, w16)
            out_a, (wo16, wup16, wdn16, wg16) = _attn_a(
                proj_t, (w_o[i], w_up[i], w_down[i], w_ple_gate[i]))
            out_b = _attn_b(proj, rel_pos_bias[i].reshape(-1))
            h = _oproj(out_a, out_b, wo16, h, post_mix_norm[i].reshape(1, d))
            h = _mlp(h, pre_mlp_norm[i].reshape(1, d), wup16, wdn16, post_mlp_norm[i].reshape(1, d))
            h = _ple(h, p[i, bi], pre_ple_norm[i].reshape(1, d), wg16,
                     w_ple_proj[i].astype(BF16), post_ple_norm[i].reshape(1, d))
        outs.append(h)
    return jnp.stack(outs, axis=0)
```

```python
import functools
import math

import jax
import jax.numpy as jnp
from jax import lax
from jax.experimental import pallas as pl
from jax.experimental.pallas import tpu as pltpu

F32 = jnp.float32
BF16 = jnp.bfloat16

HEAD_DIM = 128
GRID_W = 64
A_Q_HEADS = 8
A_KV_HEADS = 2
A_GROUP = A_Q_HEADS // A_KV_HEADS
B_HEADS = 8
WIN_ROWS = 8
WIN_COLS = 16
ROPE_THETA = 10000.0
NORM_EPS = 1e-6
LOG2E = math.log2(math.e)
SM_SCALE = 1.0 / math.sqrt(HEAD_DIM)
NEG_BIG = -1e30

QA_COL, KA_COL, VA_COL = 0, 8, 10
A_COLS = 12 * HEAD_DIM
QB_COL, KB_COL, VB_COL = 0, 8, 16

V7X_VMEM_BYTES = 64 * 1024 * 1024
VMEM_LIMIT = V7X_VMEM_BYTES - 8 * 1024 * 1024

NB_QROWS = 4
NB_KROWS = NB_QROWS + WIN_ROWS
NB_TQ = NB_QROWS * GRID_W
NB_TK = NB_KROWS * GRID_W


def _rms(x, g):
    return x * lax.rsqrt(jnp.mean(x * x, axis=-1, keepdims=True) + NORM_EPS) * g


def _inproj_a_kernel(x_ref, g_ref, wat_ref, crow_ref, srow_ref, ccol_ref, scol_ref, expand_ref,
                     qn_ref, kn_ref, oat_ref, xn_ref):
    tm = x_ref.shape[0]
    xn = _rms(x_ref[...], g_ref[...]).astype(BF16)
    xn_ref[...] = xn
    acc = lax.dot_general(wat_ref[...], xn, (((1,), (1,)), ((), ())),
                          preferred_element_type=F32)

    def per_token(row_ref, col_ref):
        return jnp.dot(row_ref[...], expand_ref[...], preferred_element_type=F32) + col_ref[...]

    cos = per_token(crow_ref, ccol_ref)
    sin = per_token(srow_ref, scol_ref)
    q4 = HEAD_DIM // 4
    reps = tm // HEAD_DIM
    for hh in range(A_COLS // HEAD_DIM):
        blk = acc[hh * HEAD_DIM:(hh + 1) * HEAD_DIM, :]
        if hh < VA_COL:
            gain = jnp.tile(qn_ref[...] if hh < KA_COL else kn_ref[...], (1, reps))
            ssq = jnp.sum(blk * blk, axis=0, keepdims=True)
            y = blk * lax.rsqrt(ssq * (1.0 / HEAD_DIM) + NORM_EPS) * gain
            partner = jnp.concatenate(
                [y[q4:2 * q4, :], y[:q4, :], y[3 * q4:, :], y[2 * q4:3 * q4, :]], axis=0)
            blk = y * cos + partner * sin
        oat_ref[hh * HEAD_DIM:(hh + 1) * HEAD_DIM, :] = blk.astype(BF16)


def _inproj_a(x, g, wat, rope, qn, kn, *, tm=1024):
    s_len, d = x.shape
    assert s_len % tm == 0 and tm % GRID_W == 0 and wat.shape == (A_COLS, d)
    crow, srow, ccol, scol, expand = rope
    tile_rows = tm // GRID_W
    const = lambda i: (0, 0)
    return pl.pallas_call(
        _inproj_a_kernel,
        grid=(s_len // tm,),
        in_specs=[
            pl.BlockSpec((tm, d), lambda i: (i, 0)),
            pl.BlockSpec((1, d), const),
            pl.BlockSpec(wat.shape, const, pipeline_mode=pl.Buffered(1)),
            pl.BlockSpec((None, HEAD_DIM, 2 * tile_rows), lambda i: (i, 0, 0)),
            pl.BlockSpec((None, HEAD_DIM, 2 * tile_rows), lambda i: (i, 0, 0)),
            pl.BlockSpec((HEAD_DIM, tm), const),
            pl.BlockSpec((HEAD_DIM, tm), const),
            pl.BlockSpec((2 * tile_rows, tm), const),
            pl.BlockSpec((HEAD_DIM, HEAD_DIM), const),
            pl.BlockSpec((HEAD_DIM, HEAD_DIM), const),
        ],
        out_specs=[pl.BlockSpec((A_COLS, tm), lambda i: (0, i)),
                   pl.BlockSpec((tm, d), lambda i: (i, 0))],
        out_shape=[jax.ShapeDtypeStruct((A_COLS, s_len), BF16),
                   jax.ShapeDtypeStruct((s_len, d), BF16)],
        compiler_params=pltpu.CompilerParams(
            dimension_semantics=("parallel",), vmem_limit_bytes=VMEM_LIMIT),
        name="inproj_a",
    )(x, g, wat, crow, srow, ccol, scol, expand, qn, kn)


def _inproj_b_kernel(xn_ref, w_ref, o_ref):
    j = pl.program_id(1)

    @pl.when(j == 0)
    def _():
        nq = B_HEADS * HEAD_DIM
        acc = jnp.dot(xn_ref[...], w_ref[...], preferred_element_type=F32)
        o_ref[:, :nq] = (acc[:, :nq] * (SM_SCALE * LOG2E)).astype(BF16)
        o_ref[:, nq:] = acc[:, nq:].astype(BF16)

    @pl.when(j > 0)
    def _():
        o_ref[...] = jnp.dot(xn_ref[...], w_ref[...], preferred_element_type=F32).astype(BF16)


def _inproj_b(xn, w, *, tm=1024, tn=A_COLS):
    s_len, d = xn.shape
    n = w.shape[1] - A_COLS
    assert s_len % tm == 0 and n % tn == 0 and A_COLS % tn == 0 and tn >= B_HEADS * HEAD_DIM
    skip = A_COLS // tn
    return pl.pallas_call(
        _inproj_b_kernel,
        grid=(s_len // tm, n // tn),
        in_specs=[
            pl.BlockSpec((tm, d), lambda i, j: (i, 0)),
            pl.BlockSpec((d, tn), lambda i, j: (0, j + skip)),
        ],
        out_specs=pl.BlockSpec((tm, tn), lambda i, j: (i, j)),
        out_shape=jax.ShapeDtypeStruct((s_len, n), BF16),
        compiler_params=pltpu.CompilerParams(
            dimension_semantics=("parallel", "arbitrary"), vmem_limit_bytes=VMEM_LIMIT),
        name="inproj_b",
    )(xn, w)


EXP2_SAFE_BOUND = 60.0


ONES_ROWS = 16


def _attn_a_kernel(*refs, tq, tk, nk, n_cast):
    q_ref, kt_ref, vt_ref = refs[:3]
    w32_refs = refs[3:3 + n_cast]
    o_ref = refs[3 + n_cast]
    w16_refs = refs[4 + n_cast:4 + 2 * n_cast]
    qt_ref, kn_ref, vxt_ref, kmax_ref, m_ref, acc_ref = refs[4 + 2 * n_cast:]
    i = pl.program_id(1)
    for w32, w16 in zip(w32_refs, w16_refs):
        w16[...] = w32[...].astype(BF16)

    @pl.when(i == 0)
    def _prepare_kv():
        row = lax.broadcasted_iota(jnp.int32, (ONES_ROWS, tk), 0)
        ones_rows = jnp.where(row == 0, 1.0, 0.0).astype(BF16)
        kmax = jnp.zeros((1, tk), F32)
        for c in range(nk):
            cols = slice(c * tk, (c + 1) * tk)
            ktf = kt_ref[:, cols].astype(F32)
            kn_ref[cols, :] = ktf.T.astype(BF16)
            vxt_ref[c, :HEAD_DIM, :] = vt_ref[:, cols]
            vxt_ref[c, HEAD_DIM:, :] = ones_rows
            kmax = jnp.maximum(kmax, jnp.sum(ktf * ktf, axis=0, keepdims=True))
        kmax_ref[0] = jnp.max(kmax)

    for h in range(A_GROUP):
        qt_ref[:, h * tq:(h + 1) * tq] = q_ref[h * HEAD_DIM:(h + 1) * HEAD_DIM, :]
    acc_ref[...] = jnp.zeros(acc_ref.shape, F32)
    qf = qt_ref[...].astype(F32)
    qmax = jnp.max(jnp.sum(qf * qf, axis=0, keepdims=True))
    no_max_needed = qmax * kmax_ref[0] <= EXP2_SAFE_BOUND * EXP2_SAFE_BOUND

    @pl.when(no_max_needed)
    def _plain():
        def body(j, carry):
            off = pl.multiple_of(j * tk, tk)
            s = jnp.dot(kn_ref[pl.ds(off, tk), :], qt_ref[...], preferred_element_type=F32)
            p = jnp.exp2(s).astype(BF16)
            acc_ref[...] += jnp.dot(vxt_ref[j], p, preferred_element_type=F32)
            return carry

        lax.fori_loop(0, nk, body, 0, unroll=16)

    @pl.when(jnp.logical_not(no_max_needed))
    def _online():
        m_ref[...] = jnp.full(m_ref.shape, -jnp.inf, F32)

        def body(j, carry):
            off = pl.multiple_of(j * tk, tk)
            s = jnp.dot(kn_ref[pl.ds(off, tk), :], qt_ref[...], preferred_element_type=F32)
            m_prev = m_ref[...]
            m_new = jnp.maximum(m_prev, jnp.max(s, axis=0, keepdims=True))
            p = jnp.exp2(s - m_new).astype(BF16)
            acc_ref[...] = jnp.exp2(m_prev - m_new) * acc_ref[...] + jnp.dot(
                vxt_ref[j], p, preferred_element_type=F32)
            m_ref[...] = m_new
            return carry

        lax.fori_loop(0, nk, body, 0)

    out_t = acc_ref[:HEAD_DIM, :] / acc_ref[HEAD_DIM:HEAD_DIM + 1, :]
    for h in range(A_GROUP):
        o_ref[:, h * HEAD_DIM:(h + 1) * HEAD_DIM] = out_t[:, h * tq:(h + 1) * tq].T.astype(BF16)


def _attn_a(proj_t, cast_ws, *, tq=512, tk=512):
    s_len = proj_t.shape[1]
    assert s_len % tq == 0 and s_len % tk == 0
    m_cols = A_GROUP * tq
    gw = A_GROUP * HEAD_DIM
    nk = s_len // tk
    n_i = s_len // tq
    n_steps = A_KV_HEADS * n_i
    bf16_rows = 16
    slabs = []
    for w in cast_ws:
        assert w.shape[0] % (n_steps * bf16_rows) == 0
        slabs.append(pl.BlockSpec((w.shape[0] // n_steps, w.shape[1]),
                                  lambda g, i: (g * n_i + i, 0)))
    kern = functools.partial(_attn_a_kernel, tq=tq, tk=tk, nk=nk, n_cast=len(cast_ws))
    outs = pl.pallas_call(
        kern,
        grid=(A_KV_HEADS, n_i),
        in_specs=[
            pl.BlockSpec((gw, tq), lambda g, i: (g, i)),
            pl.BlockSpec((HEAD_DIM, s_len), lambda g, i: (KA_COL + g, 0)),
            pl.BlockSpec((HEAD_DIM, s_len), lambda g, i: (VA_COL + g, 0)),
        ] + slabs,
        out_specs=[pl.BlockSpec((tq, gw), lambda g, i: (i, g))] + slabs,
        out_shape=[jax.ShapeDtypeStruct((s_len, A_Q_HEADS * HEAD_DIM), BF16)]
        + [jax.ShapeDtypeStruct(w.shape, BF16) for w in cast_ws],
        scratch_shapes=[
            pltpu.VMEM((HEAD_DIM, m_cols), BF16),
            pltpu.VMEM((s_len, HEAD_DIM), BF16),
            pltpu.VMEM((nk, HEAD_DIM + ONES_ROWS, tk), BF16),
            pltpu.SMEM((1,), F32),
            pltpu.VMEM((1, m_cols), F32),
            pltpu.VMEM((HEAD_DIM + ONES_ROWS, m_cols), F32),
        ],
        compiler_params=pltpu.CompilerParams(
            dimension_semantics=("arbitrary", "arbitrary"), vmem_limit_bytes=VMEM_LIMIT),
        name="attn_global",
    )(proj_t, proj_t, proj_t, *cast_ws)
    return outs[0], tuple(outs[1:])


def _nb_block_types(rows):
    last_q = rows - NB_QROWS
    return ((0, 0), (2 * NB_QROWS, 2 * NB_QROWS - WIN_ROWS // 2), (last_q, rows - NB_KROWS))


def _attn_b_kernel(rpb_ref, q_ref, k_ref, v_ref, o_ref, tcol_ref, tab_ref, vx_ref, *, rows, ub):
    h = pl.program_id(0)
    step = pl.program_id(1)
    nblk = rows // NB_QROWS
    n_dr = 2 * WIN_ROWS - 1
    n_dc = 2 * WIN_COLS - 1

    @pl.when(step == 0)
    def _build_tables():
        chunk = NB_TQ
        lane_v = lax.broadcasted_iota(jnp.int32, (chunk, HEAD_DIM), 1)
        ones_col = jnp.where(lane_v == 0, 1.0, 0.0).astype(BF16)

        def copy_v(c, carry):
            off_c = pl.multiple_of(c * chunk, chunk)
            vx_ref[pl.ds(off_c, chunk), :HEAD_DIM] = v_ref[pl.ds(off_c, chunk), :]
            vx_ref[pl.ds(off_c, chunk), HEAD_DIM:] = ones_col
            return carry

        lax.fori_loop(0, v_ref.shape[0] // chunk, copy_v, 0)

        c = lax.broadcasted_iota(jnp.int32, (GRID_W, 2 * GRID_W), 0)
        kc = lax.broadcasted_iota(jnp.int32, (GRID_W, 2 * GRID_W), 1) % GRID_W
        c0 = jnp.clip(c - WIN_COLS // 2, 0, GRID_W - WIN_COLS)
        col_ok = (kc >= c0) & (kc < c0 + WIN_COLS)
        dc = kc - c + (WIN_COLS - 1)
        base = h * (n_dr * n_dc)

        def row_body(a, carry):
            t = jnp.full((GRID_W, 2 * GRID_W), NEG_BIG, F32)
            for b in range(n_dc):
                t = jnp.where(dc == b, rpb_ref[base + a * n_dc + b] * LOG2E, t)
            tcol_ref[a] = jnp.where(col_ok, t, NEG_BIG)
            return carry

        lax.fori_loop(0, n_dr, row_body, 0)

        lane = lax.broadcasted_iota(jnp.int32, (GRID_W, 2 * GRID_W), 1)
        neg = jnp.full((GRID_W, 2 * GRID_W), NEG_BIG, F32)
        for t, (r_first, k_first) in enumerate(_nb_block_types(rows)):
            for qr in range(NB_QROWS):
                r = r_first + qr
                r0 = min(max(r - WIN_ROWS // 2, 0), rows - WIN_ROWS)
                for jj in range(NB_KROWS // 2):
                    halves = []
                    for kr in (2 * jj, 2 * jj + 1):
                        k_abs = k_first + kr
                        ok = r0 <= k_abs < r0 + WIN_ROWS
                        halves.append(tcol_ref[k_abs - r + WIN_ROWS - 1] if ok else neg)
                    tab_ref[t, qr * GRID_W:(qr + 1) * GRID_W,
                            jj * 2 * GRID_W:(jj + 1) * 2 * GRID_W] = jnp.where(
                                lane < GRID_W, halves[0], halves[1])

    for u in range(ub):
        pb = step * ub + u
        btype = jnp.where(pb == 0, 0, jnp.where(pb == nblk - 1, 2, 1))
        k_first = jnp.clip(pb * NB_QROWS - WIN_ROWS // 2, 0, rows - NB_KROWS)
        off = pl.multiple_of(k_first * GRID_W, GRID_W)
        ks = k_ref[pl.ds(off, NB_TK), :]
        s = lax.dot_general(q_ref[u * NB_TQ:(u + 1) * NB_TQ, :], ks, (((1,), (1,)), ((), ())),
                            preferred_element_type=F32)
        s = s + tab_ref[btype]
        m = jnp.max(s, axis=1, keepdims=True)
        p = jnp.exp2(s - m).astype(BF16)
        acc = jnp.dot(p, vx_ref[pl.ds(off, NB_TK), :], preferred_element_type=F32)
        o_ref[u * NB_TQ:(u + 1) * NB_TQ, :] = (
            acc[:, :HEAD_DIM] / acc[:, HEAD_DIM:HEAD_DIM + 1]).astype(BF16)


def _attn_b(proj, rpb_flat, *, max_ub=32):
    s_len = proj.shape[0]
    rows = s_len // GRID_W
    nblk = rows // NB_QROWS
    assert rows % NB_QROWS == 0 and rows >= 3 * NB_QROWS + WIN_ROWS
    ub = max(u for u in range(1, max_ub + 1) if nblk % u == 0)
    kern = functools.partial(_attn_b_kernel, rows=rows, ub=ub)
    return pl.pallas_call(
        kern,
        grid=(B_HEADS, nblk // ub),
        in_specs=[
            pl.BlockSpec(memory_space=pltpu.SMEM),
            pl.BlockSpec((ub * NB_TQ, HEAD_DIM), lambda h, p: (p, QB_COL + h)),
            pl.BlockSpec((s_len, HEAD_DIM), lambda h, p: (0, KB_COL + h)),
            pl.BlockSpec((s_len, HEAD_DIM), lambda h, p: (0, VB_COL + h)),
        ],
        out_specs=pl.BlockSpec((ub * NB_TQ, HEAD_DIM), lambda h, p: (p, h)),
        out_shape=jax.ShapeDtypeStruct((s_len, B_HEADS * HEAD_DIM), BF16),
        scratch_shapes=[
            pltpu.VMEM((2 * WIN_ROWS - 1, GRID_W, 2 * GRID_W), F32),
            pltpu.VMEM((3, NB_TQ, NB_TK), F32),
            pltpu.VMEM((s_len, 2 * HEAD_DIM), BF16),
        ],
        compiler_params=pltpu.CompilerParams(
            dimension_semantics=("parallel", "arbitrary"), vmem_limit_bytes=VMEM_LIMIT),
        name="attn_nbr",
    )(rpb_flat, proj, proj, proj)


ROW_SUB = 512


def _row_subblocks(tm):
    return [slice(r, r + ROW_SUB) for r in range(0, tm, ROW_SUB)]


def _oproj_kernel(oa_ref, ob_ref, wo_ref, x_ref, g_ref, h_ref):
    na = oa_ref.shape[1]
    for rows in _row_subblocks(x_ref.shape[0]):
        mix = jnp.dot(oa_ref[rows, :], wo_ref[:na, :], preferred_element_type=F32)
        mix = mix + jnp.dot(ob_ref[rows, :], wo_ref[na:, :], preferred_element_type=F32)
        h_ref[rows, :] = x_ref[rows, :] + _rms(mix, g_ref[...])


def _oproj(oa, ob, wo, x, g, *, tm=512):
    s_len, d = x.shape
    assert s_len % tm == 0 and tm % ROW_SUB == 0
    return pl.pallas_call(
        _oproj_kernel,
        grid=(s_len // tm,),
        in_specs=[
            pl.BlockSpec((tm, oa.shape[1]), lambda i: (i, 0)),
            pl.BlockSpec((tm, ob.shape[1]), lambda i: (i, 0)),
            pl.BlockSpec(wo.shape, lambda i: (0, 0), pipeline_mode=pl.Buffered(1)),
            pl.BlockSpec((tm, d), lambda i: (i, 0)),
            pl.BlockSpec((1, d), lambda i: (0, 0)),
        ],
        out_specs=pl.BlockSpec((tm, d), lambda i: (i, 0)),
        out_shape=jax.ShapeDtypeStruct((s_len, d), F32),
        compiler_params=pltpu.CompilerParams(
            dimension_semantics=("parallel",), vmem_limit_bytes=VMEM_LIMIT),
        name="oproj",
    )(oa, ob, wo, x, g)


def _mlp_kernel(h_ref, gpre_ref, wup_ref, wdn_ref, gpost_ref, o_ref, xn_ref, acc_ref):
    f = pl.program_id(1)

    def ff_chunk():
        u = jnp.dot(xn_ref[...], wup_ref[...], preferred_element_type=F32)
        a = jnp.square(jnp.maximum(u, 0.0)).astype(BF16)
        return jnp.dot(a, wdn_ref[...], preferred_element_type=F32)

    @pl.when(f == 0)
    def _():
        xn_ref[...] = _rms(h_ref[...], gpre_ref[...]).astype(BF16)
        acc_ref[...] = ff_chunk()

    last = pl.num_programs(1) - 1

    @pl.when((f > 0) & (f < last))
    def _():
        acc_ref[...] += ff_chunk()

    @pl.when(f == last)
    def _():
        o_ref[...] = h_ref[...] + _rms(acc_ref[...] + ff_chunk(), gpost_ref[...])


def _mlp(h, gpre, wup, wdn, gpost, *, tm=512, tf=1024):
    s_len, d = h.shape
    d_ff = wup.shape[1]
    assert s_len % tm == 0 and d_ff % tf == 0
    return pl.pallas_call(
        _mlp_kernel,
        grid=(s_len // tm, d_ff // tf),
        in_specs=[
            pl.BlockSpec((tm, d), lambda i, f: (i, 0)),
            pl.BlockSpec((1, d), lambda i, f: (0, 0)),
            pl.BlockSpec((d, tf), lambda i, f: (0, f)),
            pl.BlockSpec((tf, d), lambda i, f: (f, 0)),
            pl.BlockSpec((1, d), lambda i, f: (0, 0)),
        ],
        out_specs=pl.BlockSpec((tm, d), lambda i, f: (i, 0)),
        out_shape=jax.ShapeDtypeStruct((s_len, d), F32),
        scratch_shapes=[pltpu.VMEM((tm, d), BF16), pltpu.VMEM((tm, d), F32)],
        compiler_params=pltpu.CompilerParams(
            dimension_semantics=("parallel", "arbitrary"), vmem_limit_bytes=VMEM_LIMIT),
        name="mlp",
    )(h, gpre, wup, wdn, gpost)


def _ple_kernel(h_ref, p_ref, gpre_ref, wg_ref, wp_ref, gpost_ref, o_ref):
    for rows in _row_subblocks(h_ref.shape[0]):
        h = h_ref[rows, :]
        xn = _rms(h, gpre_ref[...]).astype(BF16)
        gate = jax.nn.sigmoid(jnp.dot(xn, wg_ref[...], preferred_element_type=F32))
        e = jnp.dot(p_ref[rows, :].astype(BF16), wp_ref[...], preferred_element_type=F32) * gate
        o_ref[rows, :] = h + _rms(e, gpost_ref[...])


def _ple(h, p, gpre, wg, wp, gpost, *, tm=512):
    s_len, d = h.shape
    dp = p.shape[1]
    assert s_len % tm == 0 and tm % ROW_SUB == 0
    return pl.pallas_call(
        _ple_kernel,
        grid=(s_len // tm,),
        in_specs=[
            pl.BlockSpec((tm, d), lambda i: (i, 0)),
            pl.BlockSpec((tm, dp), lambda i: (i, 0)),
            pl.BlockSpec((1, d), lambda i: (0, 0)),
            pl.BlockSpec(wg.shape, lambda i: (0, 0), pipeline_mode=pl.Buffered(1)),
            pl.BlockSpec(wp.shape, lambda i: (0, 0), pipeline_mode=pl.Buffered(1)),
            pl.BlockSpec((1, d), lambda i: (0, 0)),
        ],
        out_specs=pl.BlockSpec((tm, d), lambda i: (i, 0)),
        out_shape=jax.ShapeDtypeStruct((s_len, d), F32),
        compiler_params=pltpu.CompilerParams(
            dimension_semantics=("parallel",), vmem_limit_bytes=VMEM_LIMIT),
        name="ple",
    )(h, p, gpre, wg, wp, gpost)


def _rope_tables(s_len, tm):
    rows = s_len // GRID_W
    tile_rows = tm // GRID_W
    n_freq = HEAD_DIM // 4
    freqs = ROPE_THETA ** (-jnp.arange(n_freq, dtype=F32) / n_freq)
    ar = freqs[:, None] * jnp.arange(rows, dtype=F32)[None, :]
    ac = freqs[:, None] * jnp.arange(GRID_W, dtype=F32)[None, :]
    zr, zc = jnp.zeros_like(ar), jnp.zeros_like(ac)

    def row_table(t):
        hi = t.astype(BF16)
        lo = (t - hi.astype(F32)).astype(BF16)
        per_tile = lambda a: a.reshape(HEAD_DIM, rows // tile_rows, tile_rows).swapaxes(0, 1)
        return jnp.concatenate([per_tile(hi), per_tile(lo)], axis=-1)

    crow = row_table(jnp.concatenate([jnp.cos(ar), jnp.cos(ar), zr, zr], axis=0))
    srow = row_table(jnp.concatenate([-jnp.sin(ar), jnp.sin(ar), zr, zr], axis=0))
    ccol = jnp.tile(jnp.concatenate([zc, zc, jnp.cos(ac), jnp.cos(ac)], axis=0), (1, tile_rows))
    scol = jnp.tile(jnp.concatenate([zc, zc, -jnp.sin(ac), jnp.sin(ac)], axis=0), (1, tile_rows))
    tok_row = jnp.arange(tm, dtype=jnp.int32)[None, :] // GRID_W
    line = jnp.arange(2 * tile_rows, dtype=jnp.int32)[:, None] % tile_rows
    expand = (tok_row == line).astype(BF16)
    return crow, srow, ccol, scol, expand


def kernel(x, p, pre_mix_norm, w_in, q_norm, k_norm, rel_pos_bias, w_o, post_mix_norm,
           pre_mlp_norm, w_up, w_down, post_mlp_norm, pre_ple_norm, w_ple_gate, w_ple_proj,
           post_ple_norm):
    b, s_len, d = x.shape
    depth = w_in.shape[0]
    tm_in = 1024
    rope = _rope_tables(s_len, tm_in)
    outs = []
    for bi in range(b):
        h = x[bi]
        for i in range(depth):
            lanes = (HEAD_DIM, HEAD_DIM)
            qn = jnp.broadcast_to((q_norm[i] * (SM_SCALE * LOG2E))[:, None], lanes)
            kn = jnp.broadcast_to(k_norm[i][:, None], lanes)
            w16 = w_in[i].astype(BF16)
            wat = w16[:, :A_COLS].T
            proj_t, xn = _inproj_a(h, pre_mix_norm[i].reshape(1, d), wat, rope, qn, kn, tm=tm_in)
            proj = _inproj_b(xn, w16)
            out_a, (wo16, wup16, wdn16, wg16) = _attn_a(
                proj_t, (w_o[i], w_up[i], w_down[i], w_ple_gate[i]))
            out_b = _attn_b(proj, rel_pos_bias[i].reshape(-1))
            h = _oproj(out_a, out_b, wo16, h, post_mix_norm[i].reshape(1, d))
            h = _mlp(h, pre_mlp_norm[i].reshape(1, d), wup16, wdn16, post_mlp_norm[i].reshape(1, d))
            h = _ple(h, p[i, bi], pre_ple_norm[i].reshape(1, d), wg16,
                     w_ple_proj[i].astype(BF16), post_ple_norm[i].reshape(1, d))
        outs.append(h)
    return jnp.stack(outs, axis=0)
```
